```python
import jax, jax.numpy as jnp
from jax import lax
import numpy as np

D_MODEL = 1024
BATCH = 8
SEQ = 2048
DEPTH = 4
DEC_BATCH = 32
DEC_SEQ = 8
PAST_LEN = 8192
PAGE_SIZE = 128

N_A_LAYERS = DEPTH // 2
N_B_LAYERS = DEPTH - N_A_LAYERS
A_EXPAND = 128
A_HEADS = D_MODEL // A_EXPAND
A_DK = A_EXPAND
A_DV = D_MODEL // A_HEADS
A_CHUNK = 64
B_HEADS = 16
B_HEAD_DIM = D_MODEL // B_HEADS
Q_BLOCK = 128
D_FF = ((8 * D_MODEL // 3 + 255) // 256) * 256
RMS_EPS = 1e-6
MASK_VALUE = -1e30
MIN_F = 1e-30

kernel_name = 'hgrn2_fox_yoco_step'

F32 = jnp.float32


def rmsnorm(x, g):
    xf = x.astype(F32)
    y = xf * lax.rsqrt(jnp.mean(xf * xf, axis=-1, keepdims=True) + RMS_EPS)
    return (y * g.astype(F32)).astype(x.dtype)


def swiglu_ffn(x, g, w_gu, w_down):
    h = rmsnorm(x, g) @ w_gu
    a, b = jnp.split(h, 2, axis=-1)
    return (jax.nn.silu(a) * b) @ w_down


def gla_chunked(q, k, logf, v, s0):
    B, L, H, DK = q.shape
    DV = v.shape[-1]
    c = min(A_CHUNK, L)
    n = -(-L // c)
    pad = n * c - L
    if pad:
        pw = ((0, 0), (0, pad), (0, 0), (0, 0))
        q, k, logf, v = (jnp.pad(t, pw) for t in (q, k, logf, v))

    def chunks(t):
        return t.reshape(B, n, c, H, t.shape[-1]).transpose(1, 0, 3, 2, 4)

    causal = jnp.tril(jnp.ones((c, c), dtype=bool))[None, None, :, :, None]

    def step(S, inp):
        qc, kc, gc, vc = inp
        b = jnp.cumsum(gc, axis=2)
        diff = b[:, :, :, None, :] - b[:, :, None, :, :]
        decay = jnp.where(causal, jnp.exp(jnp.where(causal, diff, 0.0)), 0.0)
        att = jnp.sum(qc[:, :, :, None, :] * kc[:, :, None, :, :] * decay, axis=-1)
        o = jnp.einsum('bhts,bhsv->bhtv', att, vc) + jnp.einsum('bhtd,bhdv->bhtv', qc * jnp.exp(b), S)
        b_last = b[:, :, -1]
        S = jnp.exp(b_last)[..., None] * S + jnp.einsum(
            'bhsd,bhsv->bhdv', kc * jnp.exp(b_last[:, :, None, :] - b), vc)
        return S, o

    s_fin, o = lax.scan(step, s0, (chunks(q), chunks(k), chunks(logf), chunks(v)))
    o = o.transpose(1, 0, 3, 2, 4).reshape(B, n * c, H, DV)[:, :L]
    return o, s_fin


def hgrn2_mixer(x, s0, norm_g, w_in, lb, o_g, w_out):
    B, L, _ = x.shape
    kd = A_HEADS * A_DK
    vd = A_HEADS * A_DV
    h = rmsnorm(x, norm_g) @ w_in
    q, fz, inp, gate = jnp.split(h, [kd, 2 * kd, 2 * kd + vd], axis=-1)
    q = q.reshape(B, L, A_HEADS, A_DK).astype(F32)
    fz = fz.reshape(B, L, A_HEADS, A_DK).astype(F32)
    v = inp.reshape(B, L, A_HEADS, A_DV).astype(F32)
    lb = lb.reshape(A_HEADS, A_DK).astype(F32)
    f = lb + (1.0 - lb) * jax.nn.sigmoid(fz)
    logf = jnp.log(jnp.maximum(f, MIN_F))
    k = (1.0 - lb) * jax.nn.sigmoid(-fz)
    o, s_new = gla_chunked(q, k, logf, v, s0.astype(F32))
    o = rmsnorm(o, o_g).reshape(B, L, vd).astype(x.dtype) * jax.nn.silu(gate)
    return o @ w_out, s_new


def shared_kv(x, kv_g, w_kvf, b_f, k_g):
    B, L, _ = x.shape
    hd = B_HEADS * B_HEAD_DIM
    h = rmsnorm(x, kv_g) @ w_kvf
    k, v, fz = jnp.split(h, [hd, 2 * hd], axis=-1)
    k = rmsnorm(k.reshape(B, L, B_HEADS, B_HEAD_DIM), k_g)
    v = v.reshape(B, L, B_HEADS, B_HEAD_DIM)
    logf = jax.nn.log_sigmoid(fz.astype(F32) + b_f.astype(F32))
    return k, v, logf


def fox_attention(q, k, v, cq, ck, q_pos, k_pos):
    B, Lq, H, Dh = q.shape
    blk = min(Q_BLOCK, Lq)
    nb = -(-Lq // blk)
    pad = nb * blk - Lq
    if pad:
        q = jnp.pad(q, ((0, 0), (0, pad), (0, 0), (0, 0)))
        cq = jnp.pad(cq, ((0, 0), (0, pad), (0, 0)), mode='edge')
        q_pos = jnp.pad(q_pos, (0, pad), mode='edge')
    qb = q.reshape(B, nb, blk, H, Dh).transpose(1, 0, 2, 3, 4)
    cb = cq.reshape(B, nb, blk, H).transpose(1, 0, 3, 2)
    pb = q_pos.reshape(nb, blk)
    ckt = ck.transpose(0, 2, 1)
    scale = Dh ** -0.5

    def block(args):
        qi, ci, pi = args
        s = jnp.einsum('bqhd,bkhd->bhqk', qi, k).astype(F32) * scale
        s = s + ci[..., None] - ckt[:, :, None, :]
        s = jnp.where(k_pos[None, :] <= pi[:, None], s, MASK_VALUE)
        p = jax.nn.softmax(s, axis=-1).astype(v.dtype)
        return jnp.einsum('bhqk,bkhd->bqhd', p, v)

    o = lax.map(block, (qb, cb, pb))
    return o.transpose(1, 0, 2, 3, 4).reshape(B, nb * blk, H, Dh)[:, :Lq]


def run_trunk(x, a_states, past_k, past_v, past_logf,
              a_norm, a_w_in, a_lb_eff, a_onorm, a_w_out,
              kv_norm, w_kvf, b_f, k_norm,
              b_norm, b_wq, q_norm, b_wo,
              ffn_norm, w_gate_up, w_down):
    B, L, _ = x.shape
    past_len = 0 if past_k is None else past_k.shape[1]
    new_a = []
    for layer in range(DEPTH):
        if layer < N_A_LAYERS:
            o, s = hgrn2_mixer(x, a_states[layer], a_norm[layer], a_w_in[layer],
                               a_lb_eff[layer], a_onorm[layer], a_w_out[layer])
            new_a.append(s)
            x = x + o
        else:
            j = layer - N_A_LAYERS
            if j == 0:
                k_new, v_new, logf_new = shared_kv(x, kv_norm, w_kvf, b_f, k_norm)
                if past_k is None:
                    k_all, v_all = k_new, v_new
                    c_all = jnp.cumsum(logf_new, axis=1)
                else:
                    k_all = jnp.concatenate([past_k, k_new.astype(past_k.dtype)], axis=1)
                    v_all = jnp.concatenate([past_v, v_new.astype(past_v.dtype)], axis=1)
                    c_all = jnp.cumsum(jnp.concatenate([past_logf.astype(F32), logf_new], axis=1), axis=1)
                cq = c_all[:, past_len:]
                q_pos = past_len + jnp.arange(L, dtype=jnp.int32)
                k_pos = jnp.arange(past_len + L, dtype=jnp.int32)
            h = rmsnorm(x, b_norm[j])
            q = rmsnorm((h @ b_wq[j]).reshape(B, L, B_HEADS, B_HEAD_DIM), q_norm[j])
            o = fox_attention(q, k_all.astype(q.dtype), v_all.astype(q.dtype), cq, c_all, q_pos, k_pos)
            x = x + o.reshape(B, L, B_HEADS * B_HEAD_DIM) @ b_wo[j]
        x = x + swiglu_ffn(x, ffn_norm[layer], w_gate_up[layer], w_down[layer])
    return x, jnp.stack(new_a), k_new, v_new, logf_new


def setup_inputs(seed: int = 0) -> dict:
    key = jax.random.key(seed)
    ks = jax.random.split(key, 24)
    n_pages = PAST_LEN // PAGE_SIZE
    n_used = DEC_BATCH * n_pages
    n_pool = (5 * n_used + 3) // 4
    res = (2 * DEPTH) ** -0.5
    sd = D_MODEL ** -0.5

    def nrm(k, shape, s):
        return s * jax.random.normal(k, shape, F32)

    def gain(k, shape):
        return 1.0 + 0.05 * jax.random.normal(k, shape, F32)

    page_table = jax.random.permutation(ks[6], n_pool)[:n_used].reshape(DEC_BATCH, n_pages).astype(jnp.int32)
    return {
        'x_prompt': nrm(ks[0], (BATCH, SEQ, D_MODEL), 1.0),
        'x_sample': nrm(ks[1], (DEC_BATCH, DEC_SEQ, D_MODEL), 1.0),
        'state_hgrn': nrm(ks[2], (N_A_LAYERS, DEC_BATCH, A_HEADS, A_DK, A_DV), 0.5),
        'cache_k': nrm(ks[3], (n_pool, PAGE_SIZE, B_HEADS, B_HEAD_DIM), 1.0),
        'cache_v': nrm(ks[4], (n_pool, PAGE_SIZE, B_HEADS, B_HEAD_DIM), 1.0),
        'cache_logf': jax.nn.log_sigmoid(3.0 + jax.random.normal(ks[5], (n_pool, PAGE_SIZE, B_HEADS), F32)),
        'page_table': page_table,
        'a_norm': gain(ks[7], (N_A_LAYERS, D_MODEL)),
        'a_w_in': nrm(ks[8], (N_A_LAYERS, D_MODEL, 2 * A_HEADS * A_DK + 2 * A_HEADS * A_DV), sd),
        'a_lb': nrm(ks[9], (N_A_LAYERS, A_HEADS * A_DK), 0.5),
        'a_onorm': gain(ks[10], (N_A_LAYERS, A_DV)),
        'a_w_out': nrm(ks[11], (N_A_LAYERS, A_HEADS * A_DV, D_MODEL), sd * res),
        'kv_norm': gain(ks[12], (D_MODEL,)),
        'w_kvf': nrm(ks[13], (D_MODEL, 2 * B_HEADS * B_HEAD_DIM + B_HEADS), sd),
        'b_f': 3.0 + nrm(ks[14], (B_HEADS,), 0.1),
        'k_norm': gain(ks[15], (B_HEAD_DIM,)),
        'b_norm': gain(ks[16], (N_B_LAYERS, D_MODEL)),
        'b_wq': nrm(ks[17], (N_B_LAYERS, D_MODEL, B_HEADS * B_HEAD_DIM), sd),
        'q_norm': gain(ks[18], (N_B_LAYERS, B_HEAD_DIM)),
        'b_wo': nrm(ks[19], (N_B_LAYERS, B_HEADS * B_HEAD_DIM, D_MODEL), sd * res),
        'ffn_norm': gain(ks[20], (DEPTH, D_MODEL)),
        'w_gate_up': nrm(ks[21], (DEPTH, D_MODEL, 2 * D_FF), sd),
        'w_down': nrm(ks[22], (DEPTH, D_FF, D_MODEL), (D_FF ** -0.5) * res),
    }


def reference(x_prompt, x_sample, state_hgrn, cache_k, cache_v, cache_logf, page_table,
              a_norm, a_w_in, a_lb, a_onorm, a_w_out,
              kv_norm, w_kvf, b_f, k_norm,
              b_norm, b_wq, q_norm, b_wo,
              ffn_norm, w_gate_up, w_down):
    p = jax.nn.softmax(a_lb.astype(F32), axis=0)
    lb_eff = jnp.cumsum(p, axis=0) - p[:1]
    weights = (a_norm, a_w_in, lb_eff, a_onorm, a_w_out, kv_norm, w_kvf, b_f, k_norm,
               b_norm, b_wq, q_norm, b_wo, ffn_norm, w_gate_up, w_down)

    s0_prompt = jnp.zeros((N_A_LAYERS, x_prompt.shape[0], A_HEADS, A_DK, A_DV), F32)
    y_prompt, hgrn_prompt, k_prompt, v_prompt, logf_prompt = run_trunk(
        x_prompt, s0_prompt, None, None, None, *weights)

    db, n_pages = page_table.shape
    past_len = n_pages * cache_k.shape[1]
    past_k = cache_k[page_table].reshape(db, past_len, B_HEADS, B_HEAD_DIM)
    past_v = cache_v[page_table].reshape(db, past_len, B_HEADS, B_HEAD_DIM)
    past_logf = cache_logf[page_table].reshape(db, past_len, B_HEADS)
    y_sample, hgrn_sample, k_sample, v_sample, logf_sample = run_trunk(
        x_sample, state_hgrn, past_k, past_v, past_logf, *weights)

    return (y_prompt, y_sample, hgrn_prompt, k_prompt, v_prompt, logf_prompt,
            hgrn_sample, k_sample, v_sample, logf_sample)
```

```python
import functools

import jax
import jax.numpy as jnp
from jax import lax
from jax.experimental import pallas as pl
from jax.experimental.pallas import tpu as pltpu

F32 = jnp.float32
BF16 = jnp.bfloat16
HIGHEST = lax.Precision.HIGHEST

D_MODEL = 1024
A_HEADS = 8
A_DK = 128
A_DV = 128
A_CHUNK = 64
A_SUB = 16
B_HEADS = 16
B_HEAD_DIM = 64
PAGE_SIZE = 128
RMS_EPS = 1e-6
MASK_VALUE = -1e30
MIN_F = 1e-30
LANES = 128
VMEM_LIMIT = 56 * 1024 * 1024

_NT = (((1,), (1,)), ((), ()))
_TN = (((0,), (0,)), ((), ()))


def _params(*sem):
    return pltpu.CompilerParams(dimension_semantics=sem, vmem_limit_bytes=VMEM_LIMIT)


def _rms(x, g):
    return x * lax.rsqrt(jnp.mean(x * x, axis=-1, keepdims=True) + RMS_EPS) * g


def _resident(shape):
    return pl.BlockSpec(shape, lambda *_: (0,) * len(shape), pipeline_mode=pl.Buffered(1))


def _group_indicator(n, groups_padded, group):
    r = lax.broadcasted_iota(jnp.int32, (n, groups_padded), 0)
    c = lax.broadcasted_iota(jnp.int32, (n, groups_padded), 1)
    return (r // group == c).astype(F32)


def _head_rms(x, gain_tiled):
    n = x.shape[-1]
    g = _group_indicator(n, LANES, B_HEAD_DIM)
    ss = jnp.dot(x * x, g, preferred_element_type=F32, precision=HIGHEST)
    rs = lax.rsqrt(ss * (1.0 / B_HEAD_DIM) + RMS_EPS)
    rs_full = lax.dot_general(rs, g, _NT, preferred_element_type=F32, precision=HIGHEST)
    return x * rs_full * gain_tiled


def _norm_matmul_kernel(x_ref, g_ref, w_ref, o_ref, xn_ref):
    @pl.when(pl.program_id(1) == 0)
    def _():
        xn_ref[...] = _rms(x_ref[...], g_ref[...]).astype(BF16)

    o_ref[...] = jnp.dot(xn_ref[...], w_ref[...], preferred_element_type=F32).astype(o_ref.dtype)


def _norm_matmul(x, g, w, *, tm, tn, out_dtype=F32):
    m, k = x.shape
    n = w.shape[1]
    return pl.pallas_call(
        _norm_matmul_kernel,
        grid=(m // tm, n // tn),
        in_specs=[
            pl.BlockSpec((tm, k), lambda i, j: (i, 0)),
            pl.BlockSpec((1, k), lambda i, j: (0, 0)),
            pl.BlockSpec((k, tn), lambda i, j: (0, j)),
        ],
        out_specs=pl.BlockSpec((tm, tn), lambda i, j: (i, j)),
        out_shape=jax.ShapeDtypeStruct((m, n), out_dtype),
        scratch_shapes=[pltpu.VMEM((tm, k), BF16)],
        compiler_params=_params("parallel", "arbitrary"),
        name="norm_matmul",
    )(x, g, w)


def _matmul_res_kernel(a_ref, w_ref, r_ref, o_ref):
    o_ref[...] = r_ref[...] + jnp.dot(a_ref[...].astype(BF16), w_ref[...], preferred_element_type=F32)


def _matmul_res(a, w, res, *, tm):
    m, k = a.shape
    n = w.shape[1]
    return pl.pallas_call(
        _matmul_res_kernel,
        grid=(m // tm,),
        in_specs=[
            pl.BlockSpec((tm, k), lambda i: (i, 0)),
            _resident((k, n)),
            pl.BlockSpec((tm, n), lambda i: (i, 0)),
        ],
        out_specs=pl.BlockSpec((tm, n), lambda i: (i, 0)),
        out_shape=jax.ShapeDtypeStruct((m, n), F32),
        compiler_params=_params("parallel"),
        name="matmul_res",
    )(a, w, res)


def _ffn_kernel(x_ref, g_ref, wg_ref, wu_ref, wd_ref, o_ref, h_ref, act_ref, *, tf):
    x = x_ref[...]
    h_ref[...] = _rms(x, g_ref[...]).astype(BF16)
    for j in range(act_ref.shape[1] // tf):
        sl = slice(j * tf, (j + 1) * tf)
        a = jnp.dot(h_ref[...], wg_ref[:, sl], preferred_element_type=F32)
        b = jnp.dot(h_ref[...], wu_ref[:, sl], preferred_element_type=F32)
        act_ref[:, sl] = (jax.nn.silu(a) * b).astype(BF16)
    o_ref[...] = x + jnp.dot(act_ref[...], wd_ref[...], preferred_element_type=F32)


def _ffn(x, g, wg, wu, wd, *, tm, tf=256):
    m, d = x.shape
    dff = wg.shape[1]
    return pl.pallas_call(
        functools.partial(_ffn_kernel, tf=tf),
        grid=(m // tm,),
        in_specs=[
            pl.BlockSpec((tm, d), lambda i: (i, 0)),
            pl.BlockSpec((1, d), lambda i: (0, 0)),
            _resident((d, dff)),
            _resident((d, dff)),
            _resident((dff, d)),
        ],
        out_specs=pl.BlockSpec((tm, d), lambda i: (i, 0)),
        out_shape=jax.ShapeDtypeStruct((m, d), F32),
        scratch_shapes=[pltpu.VMEM((tm, d), BF16), pltpu.VMEM((tm, dff), BF16)],
        compiler_params=_params("parallel"),
        name="ffn",
    )(x, g, wg, wu, wd)


def _gla_chunk(q, z, v, lb, st, b_scr, k_scr, v_scr, *, c, u):
    sig = jax.nn.sigmoid(z)
    f = lb + (1.0 - lb) * sig
    logf = jnp.log(jnp.maximum(f, MIN_F))
    k = (1.0 - lb) * jax.nn.sigmoid(-z)
    tri = (lax.broadcasted_iota(jnp.int32, (c, c), 0) >= lax.broadcasted_iota(jnp.int32, (c, c), 1)).astype(F32)
    b = jnp.dot(tri, logf, preferred_element_type=F32, precision=HIGHEST)
    b_scr[...] = b
    k_scr[...] = k
    v_scr[...] = v
    vb = v.astype(BF16)
    n_sub = c // u
    rows = lax.broadcasted_iota(jnp.int32, (u, 1), 0)

    o_state = lax.dot_general((q * jnp.exp(b)).astype(BF16), st.astype(BF16), _NT, preferred_element_type=F32)

    parts = []
    for i in range(n_sub):
        lo = u * i
        qi = q[lo:lo + u]
        bi = b[lo:lo + u]
        acc = jnp.zeros((u, A_DV), F32)
        for sl in range(u):
            s = lo + sl
            bs = b_scr[s:s + 1, :]
            e = jnp.exp(jnp.minimum(bi - bs, 0.0))
            a = jnp.sum(qi * k_scr[s:s + 1, :] * e, axis=-1, keepdims=True)
            a = jnp.where(rows >= sl, a, 0.0)
            acc = acc + a * v_scr[s:s + 1, :]
        if i > 0:
            r = b_scr[lo - 1:lo, :]
            qt = (qi * jnp.exp(bi - r)).astype(BF16)
            kt = (k[:lo] * jnp.exp(r - b[:lo])).astype(BF16)
            att = lax.dot_general(qt, kt, _NT, preferred_element_type=F32)
            acc = acc + jnp.dot(att.astype(BF16), vb[:lo], preferred_element_type=F32)
        parts.append(acc)
    o = (parts[0] if n_sub == 1 else jnp.concatenate(parts, axis=0)) + o_state

    bl = b_scr[c - 1:c, :]
    kd = (k * jnp.exp(bl - b)).astype(BF16)
    st_new = st * jnp.exp(bl) + lax.dot_general(vb, kd, _TN, preferred_element_type=F32)
    return o, st_new


def _gla_kernel(*refs, layer, c, u, n_chunks, hps, has_s0):
    if has_s0:
        q_ref, z_ref, v_ref, gt_ref, lb_ref, og_ref, s0_ref, o_ref, so_ref, st_ref, b_scr, k_scr, v_scr = refs
    else:
        q_ref, z_ref, v_ref, gt_ref, lb_ref, og_ref, o_ref, so_ref, st_ref, b_scr, k_scr, v_scr = refs
        s0_ref = None
    i = pl.program_id(2)

    @pl.when(i == 0)
    def _():
        for hh in range(hps):
            st_ref[hh] = s0_ref[0, hh].T if has_s0 else jnp.zeros((A_DV, A_DK), F32)

    a = lb_ref[...]
    e = jnp.exp(a - jnp.max(a, axis=0, keepdims=True))
    p = e / jnp.sum(e, axis=0, keepdims=True)
    lb_all = jnp.sum(p[:layer + 1], axis=0, keepdims=True) - p[0:1]
    og = og_ref[...]

    for hh in range(hps):
        cs = slice(hh * LANES, (hh + 1) * LANES)
        lb = lb_all[:, cs]

        def body(ci, carry, cs=cs, lb=lb, hh=hh):
            r0 = pl.multiple_of(ci * c, c)
            rs = pl.ds(r0, c)
            o, st_new = _gla_chunk(q_ref[rs, cs], z_ref[rs, cs], v_ref[rs, cs], lb, st_ref[hh],
                                   b_scr, k_scr, v_scr, c=c, u=u)
            st_ref[hh] = st_new
            on = o * lax.rsqrt(jnp.mean(o * o, axis=-1, keepdims=True) + RMS_EPS) * og
            o_ref[rs, cs] = (on * jax.nn.silu(gt_ref[rs, cs])).astype(o_ref.dtype)
            return carry

        if n_chunks == 1:
            body(0, 0)
        else:
            lax.fori_loop(0, n_chunks, body, 0)

    @pl.when(i == pl.num_programs(2) - 1)
    def _():
        for hh in range(hps):
            so_ref[0, hh] = st_ref[hh].T


def _gla(h, a_lb, o_gain, s0, *, layer, batch, seq, c, u, t, hps, out_dtype):
    m = batch * seq
    nt = seq // t
    hb = A_HEADS // hps
    w = hps * LANES
    kd = A_HEADS * A_DK
    nblk = kd // w

    def col(off):
        return lambda b, hg, i: (b * nt + i, off * nblk + hg)

    in_specs = [
        pl.BlockSpec((t, w), col(0)),
        pl.BlockSpec((t, w), col(1)),
        pl.BlockSpec((t, w), col(2)),
        pl.BlockSpec((t, w), col(3)),
        pl.BlockSpec((a_lb.shape[0], w), lambda b, hg, i: (0, hg)),
        pl.BlockSpec((1, A_DV), lambda b, hg, i: (0, 0)),
    ]
    args = [h, h, h, h, a_lb, o_gain]
    if s0 is not None:
        in_specs.append(pl.BlockSpec((1, hps, A_DK, A_DV), lambda b, hg, i: (b, hg, 0, 0)))
        args.append(s0)
    kern = functools.partial(_gla_kernel, layer=layer, c=c, u=u, n_chunks=t // c, hps=hps, has_s0=s0 is not None)
    return pl.pallas_call(
        kern,
        grid=(batch, hb, nt),
        in_specs=in_specs,
        out_specs=[
            pl.BlockSpec((t, w), lambda b, hg, i: (b * nt + i, hg)),
            pl.BlockSpec((1, hps, A_DK, A_DV), lambda b, hg, i: (b, hg, 0, 0)),
        ],
        out_shape=[
            jax.ShapeDtypeStruct((m, kd), out_dtype),
            jax.ShapeDtypeStruct((batch, A_HEADS, A_DK, A_DV), F32),
        ],
        scratch_shapes=[
            pltpu.VMEM((hps, A_DV, A_DK), F32),
            pltpu.VMEM((c, A_DK), F32),
            pltpu.VMEM((c, A_DK), F32),
            pltpu.VMEM((c, A_DV), F32),
        ],
        compiler_params=_params("parallel", "parallel", "arbitrary"),
        name="hgrn2_gla",
    )(*args)


def _kv_kernel(x_ref, g_ref, wkt_ref, wvt_ref, wft_ref, bf_ref, kg_ref, *out_refs, with_rows):
    kt_ref, vt_ref, lft_ref, kbt_ref, vb_ref = out_refs[:5]
    h = _rms(x_ref[...], g_ref[...]).astype(BF16)
    tm = h.shape[0]
    kt = lax.dot_general(wkt_ref[...], h, _NT, preferred_element_type=F32)
    k3 = kt.reshape(B_HEADS, B_HEAD_DIM, tm)
    ms = jnp.mean(k3 * k3, axis=1, keepdims=True)
    k3 = k3 * lax.rsqrt(ms + RMS_EPS) * kg_ref[...][None]
    vt = lax.dot_general(wvt_ref[...], h, _NT, preferred_element_type=F32)
    vrow = lax.dot_general(h, wvt_ref[...], _NT, preferred_element_type=F32)
    fz = lax.dot_general(wft_ref[...], h, _NT, preferred_element_type=F32) + bf_ref[...]
    kt_ref[0] = k3
    vt_ref[0] = vt.reshape(B_HEADS, B_HEAD_DIM, tm)
    lft_ref[0] = jnp.minimum(fz, 0.0) - jnp.log1p(jnp.exp(-jnp.abs(fz)))
    kbt_ref[0] = k3.astype(BF16)
    vb_ref[...] = vrow.astype(BF16)
    if with_rows:
        krow_ref, vrow_ref = out_refs[5:]
        krow_ref[...] = k3.reshape(B_HEADS * B_HEAD_DIM, tm).T
        vrow_ref[...] = vrow


def _shared_kv(x, g, wkt, wvt, wft, bf, kg, *, batch, seq, tm, with_rows):
    m, d = x.shape
    hd = B_HEADS * B_HEAD_DIM
    nt = seq // tm
    t_spec = pl.BlockSpec((1, B_HEADS, B_HEAD_DIM, tm), lambda b, i: (b, 0, 0, i))
    row_spec = pl.BlockSpec((tm, hd), lambda b, i: (b * nt + i, 0))
    t_shape = (batch, B_HEADS, B_HEAD_DIM, seq)
    out_specs = [t_spec, t_spec, pl.BlockSpec((1, B_HEADS, tm), lambda b, i: (b, 0, i)), t_spec, row_spec]
    out_shape = [
        jax.ShapeDtypeStruct(t_shape, F32),
        jax.ShapeDtypeStruct(t_shape, F32),
        jax.ShapeDtypeStruct((batch, B_HEADS, seq), F32),
        jax.ShapeDtypeStruct(t_shape, BF16),
        jax.ShapeDtypeStruct((m, hd), BF16),
    ]
    if with_rows:
        out_specs += [row_spec, row_spec]
        out_shape += [jax.ShapeDtypeStruct((m, hd), F32)] * 2
    return pl.pallas_call(
        functools.partial(_kv_kernel, with_rows=with_rows),
        grid=(batch, nt),
        in_specs=[
            pl.BlockSpec((tm, d), lambda b, i: (b * nt + i, 0)),
            pl.BlockSpec((1, d), lambda b, i: (0, 0)),
            _resident((hd, d)),
            _resident((hd, d)),
            _resident((B_HEADS, d)),
            pl.BlockSpec((B_HEADS, 1), lambda b, i: (0, 0)),
            pl.BlockSpec((B_HEAD_DIM, 1), lambda b, i: (0, 0)),
        ],
        out_specs=out_specs,
        out_shape=out_shape,
        compiler_params=_params("parallel", "parallel"),
        name="shared_kv",
    )(x, g, wkt, wvt, wft, bf, kg)


def _cumsum_prompt_kernel(x_ref, row_ref, col_ref, carry_scr):
    @pl.when(pl.program_id(1) == 0)
    def _():
        carry_scr[...] = jnp.zeros_like(carry_scr)

    x = x_ref[0]
    t = x.shape[1]
    upper = (lax.broadcasted_iota(jnp.int32, (t, t), 0) <= lax.broadcasted_iota(jnp.int32, (t, t), 1)).astype(F32)
    row = jnp.dot(x, upper, preferred_element_type=F32, precision=HIGHEST) + carry_scr[...]
    row_ref[0] = row
    col_ref[0] = row.T
    carry_scr[...] = row[:, t - 1:t]


def _cumsum_prompt(lft, *, t=256):
    b, h, l = lft.shape
    return pl.pallas_call(
        _cumsum_prompt_kernel,
        grid=(b, l // t),
        in_specs=[pl.BlockSpec((1, h, t), lambda i, j: (i, 0, j))],
        out_specs=[
            pl.BlockSpec((1, h, t), lambda i, j: (i, 0, j)),
            pl.BlockSpec((1, t, h), lambda i, j: (i, j, 0)),
        ],
        out_shape=[jax.ShapeDtypeStruct((b, h, l), F32), jax.ShapeDtypeStruct((b, l, h), F32)],
        scratch_shapes=[pltpu.VMEM((h, 1), F32)],
        compiler_params=_params("parallel", "arbitrary"),
        name="cumsum_prompt",
    )(lft)


def _fox_prefill_kernel(hq_ref, qg_ref, kt_ref, v_ref, ccol_ref, crow_ref, o_ref,
                        q_scr, m_scr, l_scr, acc_scr, *, tq):
    pair = pl.program_id(1)
    qi = pl.program_id(2)
    lane = lax.broadcasted_iota(jnp.int32, (1, LANES), 1)
    first = lane < B_HEAD_DIM

    q = hq_ref[...]
    q2 = q * q
    ss0 = jnp.sum(jnp.where(first, q2, 0.0), axis=-1, keepdims=True)
    ss1 = jnp.sum(jnp.where(first, 0.0, q2), axis=-1, keepdims=True)
    rs = jnp.where(first, lax.rsqrt(ss0 * (1.0 / B_HEAD_DIM) + RMS_EPS), lax.rsqrt(ss1 * (1.0 / B_HEAD_DIM) + RMS_EPS))
    qn = q * rs * qg_ref[...] * (B_HEAD_DIM ** -0.5)
    q_scr[0] = jnp.where(first, qn, 0.0).astype(BF16)
    q_scr[1] = jnp.where(first, 0.0, qn).astype(BF16)

    hl = lax.broadcasted_iota(jnp.int32, (1, B_HEADS), 1)
    ccol = ccol_ref[0]
    cq = [jnp.sum(jnp.where(hl == 2 * pair + hh, ccol, 0.0), axis=-1, keepdims=True) for hh in range(2)]

    m_scr[...] = jnp.full_like(m_scr, -jnp.inf)
    l_scr[...] = jnp.zeros_like(l_scr)
    acc_scr[...] = jnp.zeros_like(acc_scr)

    def step(kb, masked):
        ks = pl.ds(pl.multiple_of(kb * tq, tq), tq)
        kt = kt_ref[0, 0, :, ks]
        v = v_ref[ks, :]
        for hh in range(2):
            s = jnp.dot(q_scr[hh], kt, preferred_element_type=F32)
            s = s + (cq[hh] - crow_ref[0, 0, hh:hh + 1, ks])
            if masked:
                r = lax.broadcasted_iota(jnp.int32, (tq, tq), 0)
                c = lax.broadcasted_iota(jnp.int32, (tq, tq), 1)
                s = jnp.where(c <= r, s, MASK_VALUE)
            m_old = m_scr[hh]
            m_new = jnp.maximum(m_old, jnp.max(s, axis=-1, keepdims=True))
            alpha = jnp.exp(m_old - m_new)
            p = jnp.exp(s - m_new)
            l_scr[hh] = alpha * l_scr[hh] + jnp.sum(p, axis=-1, keepdims=True)
            acc_scr[hh] = alpha * acc_scr[hh] + jnp.dot(p.astype(BF16), v, preferred_element_type=F32)
            m_scr[hh] = m_new

    def body(kb, carry):
        step(kb, False)
        return carry

    lax.fori_loop(0, qi, body, 0)
    step(qi, True)
    o0 = acc_scr[0] / l_scr[0]
    o1 = acc_scr[1] / l_scr[1]
    o_ref[...] = jnp.where(first, o0, o1).astype(o_ref.dtype)


def _fox_prefill(hq, qg2, kbt, vb, ccol, crow, *, batch, seq, tq):
    m = batch * seq
    nq = seq // tq
    npair = B_HEADS // 2
    kbt4 = kbt.reshape(batch, npair, 2 * B_HEAD_DIM, seq)
    crow4 = crow.reshape(batch, npair, 2, seq)
    return pl.pallas_call(
        functools.partial(_fox_prefill_kernel, tq=tq),
        grid=(batch, npair, nq),
        in_specs=[
            pl.BlockSpec((tq, LANES), lambda b, p, i: (b * nq + i, p)),
            pl.BlockSpec((1, LANES), lambda b, p, i: (0, 0)),
            pl.BlockSpec((1, 1, 2 * B_HEAD_DIM, seq), lambda b, p, i: (b, p, 0, 0)),
            pl.BlockSpec((seq, LANES), lambda b, p, i: (b, p)),
            pl.BlockSpec((1, tq, B_HEADS), lambda b, p, i: (b, i, 0)),
            pl.BlockSpec((1, 1, 2, seq), lambda b, p, i: (b, p, 0, 0)),
        ],
        out_specs=pl.BlockSpec((tq, LANES), lambda b, p, i: (b * nq + i, p)),
        out_shape=jax.ShapeDtypeStruct((m, B_HEADS * B_HEAD_DIM), BF16),
        scratch_shapes=[
            pltpu.VMEM((2, tq, LANES), BF16),
            pltpu.VMEM((2, tq, 1), F32),
            pltpu.VMEM((2, tq, 1), F32),
            pltpu.VMEM((2, tq, LANES), F32),
        ],
        compiler_params=_params("parallel", "parallel", "arbitrary"),
        name="fox_prefill",
    )(hq, qg2, kbt4, vb, ccol, crow4)


def _cumsum_sample_kernel(pt_ref, page_ref, new_ref, row_ref, carry_scr, *, n_pages):
    del pt_ref
    j = pl.program_id(1)

    @pl.when(j == 0)
    def _():
        carry_scr[...] = jnp.zeros_like(carry_scr)

    def emit(x):
        t = x.shape[1]
        upper = (lax.broadcasted_iota(jnp.int32, (t, PAGE_SIZE), 0)
                 <= lax.broadcasted_iota(jnp.int32, (t, PAGE_SIZE), 1)).astype(F32)
        row = jnp.dot(x, upper, preferred_element_type=F32, precision=HIGHEST) + carry_scr[...]
        row_ref[0] = row
        carry_scr[...] = row[:, PAGE_SIZE - 1:PAGE_SIZE]

    @pl.when(j < n_pages)
    def _():
        emit(page_ref[0])

    @pl.when(j == n_pages)
    def _():
        emit(new_ref[0])


def _cumsum_sample(page_table_flat, cache_lft, lft_new, *, batch, n_pages, dec_seq):
    h = cache_lft.shape[1]
    grid_spec = pltpu.PrefetchScalarGridSpec(
        num_scalar_prefetch=1,
        grid=(batch, n_pages + 1),
        in_specs=[
            pl.BlockSpec((1, h, PAGE_SIZE), lambda b, j, pt: (pt[b * n_pages + jnp.minimum(j, n_pages - 1)], 0, 0)),
            pl.BlockSpec((1, h, dec_seq), lambda b, j, pt: (b, 0, 0)),
        ],
        out_specs=pl.BlockSpec((1, h, PAGE_SIZE), lambda b, j, pt: (b, 0, j)),
        scratch_shapes=[pltpu.VMEM((h, 1), F32)],
    )
    return pl.pallas_call(
        functools.partial(_cumsum_sample_kernel, n_pages=n_pages),
        grid_spec=grid_spec,
        out_shape=jax.ShapeDtypeStruct((batch, h, (n_pages + 1) * PAGE_SIZE), F32),
        compiler_params=_params("parallel", "arbitrary"),
        name="cumsum_sample",
    )(page_table_flat, cache_lft, lft_new)


def _fox_decode_kernel(*refs, ppb, dec_seq, n_steps):
    hq_ref, qg_ref = refs[1:3]
    k_refs = refs[3:3 + ppb]
    v_refs = refs[3 + ppb:3 + 2 * ppb]
    ck_ref, ckn_ref, cq_ref, kn_ref, vn_ref, o_ref = refs[3 + 2 * ppb:9 + 2 * ppb]
    qbd_scr, m_scr, l_scr, acc_scr, bias_scr, knp_scr, vnp_scr = refs[9 + 2 * ppb:]
    s_id = pl.program_id(1)
    hd = B_HEADS * B_HEAD_DIM
    nrow = B_HEADS * dec_seq
    lane_head = lax.broadcasted_iota(jnp.int32, (1, hd), 1) // B_HEAD_DIM

    @pl.when(s_id == 0)
    def _():
        qn = _head_rms(hq_ref[...], qg_ref[...]) * (B_HEAD_DIM ** -0.5)
        for h2 in range(B_HEADS // 2):
            a = jnp.where(lane_head == 2 * h2, qn, 0.0)
            b = jnp.where(lane_head == 2 * h2 + 1, qn, 0.0)
            qbd_scr[2 * h2 * dec_seq:(2 * h2 + 2) * dec_seq, :] = jnp.concatenate([a, b], axis=0).astype(BF16)
        m_scr[...] = jnp.full_like(m_scr, -jnp.inf)
        l_scr[...] = jnp.zeros_like(l_scr)
        acc_scr[...] = jnp.zeros_like(acc_scr)
        knp_scr[...] = jnp.zeros_like(knp_scr)
        vnp_scr[...] = jnp.zeros_like(vnp_scr)
        knp_scr[:, 0:dec_seq] = kn_ref[0]
        vnp_scr[:, 0:dec_seq] = vn_ref[0]

    cq = cq_ref[0]

    def attend(kt_tiles, vt_tiles, ck_rows, mask):
        n = len(kt_tiles)
        qbd = qbd_scr[...]
        s = jnp.concatenate([jnp.dot(qbd, kt, preferred_element_type=F32) for kt in kt_tiles], axis=1)
        for h in range(B_HEADS):
            rs = slice(h * dec_seq, (h + 1) * dec_seq)
            bias_scr[rs, 0:n * PAGE_SIZE] = cq[rs] - ck_rows[h:h + 1, :]
        s = s + bias_scr[:, 0:n * PAGE_SIZE]
        if mask is not None:
            s = jnp.where(mask, s, MASK_VALUE)
        m_old = m_scr[...]
        m_new = jnp.maximum(m_old, jnp.max(s, axis=-1, keepdims=True))
        alpha = jnp.exp(m_old - m_new)
        p = jnp.exp(s - m_new)
        l_scr[...] = alpha * l_scr[...] + jnp.sum(p, axis=-1, keepdims=True)
        pv = None
        for jj in range(n):
            t = lax.dot_general(p[:, jj * PAGE_SIZE:(jj + 1) * PAGE_SIZE].astype(BF16), vt_tiles[jj], _NT,
                                preferred_element_type=F32)
            pv = t if pv is None else pv + t
        acc_scr[...] = alpha * acc_scr[...] + pv
        m_scr[...] = m_new

    attend([r[0].reshape(hd, PAGE_SIZE).astype(BF16) for r in k_refs],
           [r[0].reshape(hd, PAGE_SIZE).astype(BF16) for r in v_refs], ck_ref[0], None)

    @pl.when(s_id == n_steps - 1)
    def _():
        r = lax.broadcasted_iota(jnp.int32, (nrow, PAGE_SIZE), 0) % dec_seq
        c = lax.broadcasted_iota(jnp.int32, (nrow, PAGE_SIZE), 1)
        attend([knp_scr[...].astype(BF16)], [vnp_scr[...].astype(BF16)], ckn_ref[0], c <= r)
        o = acc_scr[...] / l_scr[...]
        out = jnp.zeros((dec_seq, hd), F32)
        for h in range(B_HEADS):
            out = out + jnp.where(lane_head == h, o[h * dec_seq:(h + 1) * dec_seq, :], 0.0)
        o_ref[...] = out.astype(o_ref.dtype)


def _fox_decode(page_table_flat, hq, qg_full, cache_kt, cache_vt, ck_rows, cq_col, kt_new, vt_new,
                *, batch, n_pages, dec_seq, ppb):
    hd = B_HEADS * B_HEAD_DIM
    n_steps = n_pages // ppb
    nrow = B_HEADS * dec_seq

    def page(jj):
        return lambda b, s, pt: (pt[b * n_pages + s * ppb + jj], 0, 0, 0)

    page_block = (1, B_HEADS, B_HEAD_DIM, PAGE_SIZE)
    in_specs = [
        pl.BlockSpec((dec_seq, hd), lambda b, s, pt: (b, 0)),
        pl.BlockSpec((1, hd), lambda b, s, pt: (0, 0)),
    ]
    in_specs += [pl.BlockSpec(page_block, page(jj)) for jj in range(ppb)]
    in_specs += [pl.BlockSpec(page_block, page(jj)) for jj in range(ppb)]
    in_specs += [
        pl.BlockSpec((1, B_HEADS, ppb * PAGE_SIZE), lambda b, s, pt: (b, 0, s)),
        pl.BlockSpec((1, B_HEADS, PAGE_SIZE), lambda b, s, pt: (b, 0, n_pages)),
        pl.BlockSpec((1, nrow, 1), lambda b, s, pt: (b, 0, 0)),
        pl.BlockSpec((1, hd, dec_seq), lambda b, s, pt: (b, 0, 0)),
        pl.BlockSpec((1, hd, dec_seq), lambda b, s, pt: (b, 0, 0)),
    ]
    grid_spec = pltpu.PrefetchScalarGridSpec(
        num_scalar_prefetch=1,
        grid=(batch, n_steps),
        in_specs=in_specs,
        out_specs=pl.BlockSpec((dec_seq, hd), lambda b, s, pt: (b, 0)),
        scratch_shapes=[
            pltpu.VMEM((nrow, hd), BF16),
            pltpu.VMEM((nrow, 1), F32),
            pltpu.VMEM((nrow, 1), F32),
            pltpu.VMEM((nrow, hd), F32),
            pltpu.VMEM((nrow, ppb * PAGE_SIZE), F32),
            pltpu.VMEM((hd, PAGE_SIZE), F32),
            pltpu.VMEM((hd, PAGE_SIZE), F32),
        ],
    )
    args = ([page_table_flat, hq, qg_full] + [cache_kt] * ppb + [cache_vt] * ppb
            + [ck_rows, ck_rows, cq_col, kt_new, vt_new])
    return pl.pallas_call(
        functools.partial(_fox_decode_kernel, ppb=ppb, dec_seq=dec_seq, n_steps=n_steps),
        grid_spec=grid_spec,
        out_shape=jax.ShapeDtypeStruct((batch * dec_seq, hd), F32),
        compiler_params=_params("parallel", "arbitrary"),
        name="fox_decode",
    )(*args)


def _trunk(x, batch, seq, states, w, attention, *, tm, kv_tm, kv_rows, gla_cfg):
    n_a = w["a_w_in"].shape[0]
    depth = w["wg"].shape[0]
    new_states = []
    kv = ctx = None
    for layer in range(depth):
        if layer < n_a:
            h = _norm_matmul(x, w["a_norm"][layer:layer + 1], w["a_w_in"][layer], tm=tm, tn=1024)
            s0 = None if states is None else states[layer]
            og, s_new = _gla(h, w["a_lb"], w["a_onorm"][layer:layer + 1], s0, layer=layer, batch=batch, seq=seq, **gla_cfg)
            new_states.append(s_new)
            x = _matmul_res(og, w["a_w_out"][layer], x, tm=tm)
        else:
            j = layer - n_a
            if j == 0:
                kv = _shared_kv(x, w["kv_norm"], w["wkt"], w["wvt"], w["wft"], w["bf"], w["kg"],
                                batch=kv_rows[0], seq=kv_rows[1], tm=kv_tm, with_rows=attention.with_rows)
                ctx = attention.prepare(kv)
            hq = _norm_matmul(x, w["b_norm"][j:j + 1], w["b_wq"][j], tm=tm, tn=1024)
            o = attention.run(j, hq, ctx, kv)
            x = _matmul_res(o, w["b_wo"][j], x, tm=tm)
        x = _ffn(x, w["ffn_norm"][layer:layer + 1], w["wg"][layer], w["wu"][layer], w["wd"][layer], tm=min(tm, 512))
    return x, jnp.stack(new_states), kv


class _PromptAttention:
    with_rows = False

    def __init__(self, batch, seq, qg2, tq):
        self.batch, self.seq, self.qg2, self.tq = batch, seq, qg2, tq

    def prepare(self, kv):
        return _cumsum_prompt(kv[2])

    def run(self, j, hq, ctx, kv):
        crow, ccol = ctx
        return _fox_prefill(hq, self.qg2[j:j + 1], kv[3], kv[4], ccol, crow, batch=self.batch, seq=self.seq, tq=self.tq)


class _SampleAttention:
    with_rows = True

    def __init__(self, batch, dec_seq, n_pages, pt_flat, cache_kt, cache_vt, cache_lft, qg_full, ppb):
        self.batch, self.dec_seq, self.n_pages, self.pt = batch, dec_seq, n_pages, pt_flat
        self.cache_kt, self.cache_vt, self.cache_lft, self.qg_full, self.ppb = cache_kt, cache_vt, cache_lft, qg_full, ppb

    def _per_batch(self, t):
        rows = t.shape[0]
        return t.reshape(rows, self.batch, self.dec_seq).transpose(1, 0, 2)

    def prepare(self, kv):
        hd = B_HEADS * B_HEAD_DIM
        lft_new = self._per_batch(kv[2][0])
        ck_rows = _cumsum_sample(self.pt, self.cache_lft, lft_new, batch=self.batch, n_pages=self.n_pages,
                                 dec_seq=self.dec_seq)
        past = self.n_pages * PAGE_SIZE
        cq_col = ck_rows[:, :, past:past + self.dec_seq].reshape(self.batch, B_HEADS * self.dec_seq, 1)
        kt_new = self._per_batch(kv[0].reshape(hd, -1))
        vt_new = self._per_batch(kv[1].reshape(hd, -1))
        return ck_rows, cq_col, kt_new, vt_new

    def run(self, j, hq, ctx, kv):
        ck_rows, cq_col, kt_new, vt_new = ctx
        return _fox_decode(self.pt, hq, self.qg_full[j:j + 1], self.cache_kt, self.cache_vt, ck_rows, cq_col,
                           kt_new, vt_new, batch=self.batch, n_pages=self.n_pages, dec_seq=self.dec_seq, ppb=self.ppb)


def kernel(x_prompt, x_sample, state_hgrn, cache_k, cache_v, cache_logf, page_table, a_norm, a_w_in, a_lb, a_onorm, a_w_out, kv_norm, w_kvf, b_f, k_norm, b_norm, b_wq, q_norm, b_wo, ffn_norm, w_gate_up, w_down):
    batch, seq, d = x_prompt.shape
    dec_batch, dec_seq, _ = x_sample.shape
    n_pages = page_table.shape[1]
    hd = B_HEADS * B_HEAD_DIM
    dff = w_down.shape[1]
    assert seq % A_CHUNK == 0 and dec_seq <= A_CHUNK

    w_kvf_t = w_kvf.T.astype(BF16)
    w = dict(
        a_norm=a_norm, a_w_in=a_w_in.astype(BF16), a_lb=a_lb.astype(F32), a_onorm=a_onorm,
        a_w_out=a_w_out.astype(BF16), kv_norm=kv_norm.reshape(1, d),
        wkt=w_kvf_t[:hd], wvt=w_kvf_t[hd:2 * hd], wft=w_kvf_t[2 * hd:],
        bf=b_f.astype(F32).reshape(B_HEADS, 1), kg=k_norm.astype(F32).reshape(B_HEAD_DIM, 1),
        b_norm=b_norm, b_wq=b_wq.astype(BF16), b_wo=b_wo.astype(BF16), ffn_norm=ffn_norm,
        wg=w_gate_up[:, :, :dff].astype(BF16), wu=w_gate_up[:, :, dff:].astype(BF16), wd=w_down.astype(BF16),
    )
    qg2 = jnp.tile(q_norm, (1, 2))
    qg_full = jnp.tile(q_norm, (1, B_HEADS))

    prompt_att = _PromptAttention(batch, seq, qg2, tq=256)
    y_p, hgrn_p, kv_p = _trunk(
        x_prompt.reshape(batch * seq, d), batch, seq, None, w, prompt_att, tm=1024, kv_tm=512, kv_rows=(batch, seq),
        gla_cfg=dict(c=A_CHUNK, u=A_SUB, t=256, hps=1, out_dtype=BF16))
    k_p = kv_p[0].transpose(0, 3, 1, 2)
    v_p = kv_p[1].transpose(0, 3, 1, 2)
    lf_p = kv_p[2].transpose(0, 2, 1)

    m_s = dec_batch * dec_seq
    pt_flat = page_table.reshape(-1).astype(jnp.int32)
    sample_att = _SampleAttention(dec_batch, dec_seq, n_pages, pt_flat,
                                  cache_k.transpose(0, 2, 3, 1), cache_v.transpose(0, 2, 3, 1),
                                  cache_logf.transpose(0, 2, 1), qg_full, ppb=4)
    y_s, hgrn_s, kv_s = _trunk(
        x_sample.reshape(m_s, d), dec_batch, dec_seq, state_hgrn, w, sample_att, tm=m_s, kv_tm=m_s, kv_rows=(1, m_s),
        gla_cfg=dict(c=dec_seq, u=min(A_SUB, dec_seq), t=dec_seq, hps=A_HEADS, out_dtype=F32))
    k_s = kv_s[5].reshape(dec_batch, dec_seq, B_HEADS, B_HEAD_DIM)
    v_s = kv_s[6].reshape(dec_batch, dec_seq, B_HEADS, B_HEAD_DIM)
    lf_s = kv_s[2][0].T.reshape(dec_batch, dec_seq, B_HEADS)

    return (y_p.reshape(batch, seq, d), y_s.reshape(dec_batch, dec_seq, d), hgrn_p, k_p, v_p, lf_p,
            hgrn_s, k_s, v_s, lf_s)
```

```python
import functools

import jax
import jax.numpy as jnp
from jax import lax
from jax.experimental import pallas as pl
from jax.experimental.pallas import tpu as pltpu

F32 = jnp.float32
BF16 = jnp.bfloat16
HIGHEST = lax.Precision.HIGHEST

D_MODEL = 1024
A_HEADS = 8
A_DK = 128
A_DV = 128
A_CHUNK = 64
A_SUB = 16
B_HEADS = 16
B_HEAD_DIM = 64
PAGE_SIZE = 128
RMS_EPS = 1e-6
MASK_VALUE = -1e30
MIN_F = 1e-30
LOG2E = 1.4426950408889634
LANES = 128
SUBLANES = 8
VMEM_LIMIT = 56 * 1024 * 1024

_NT = (((1,), (1,)), ((), ()))
_TN = (((0,), (0,)), ((), ()))


def _params(*sem):
    return pltpu.CompilerParams(dimension_semantics=sem, vmem_limit_bytes=VMEM_LIMIT)


def _rms(x, g):
    return x * lax.rsqrt(jnp.mean(x * x, axis=-1, keepdims=True) + RMS_EPS) * g


def _resident(shape):
    return pl.BlockSpec(shape, lambda *_: (0,) * len(shape), pipeline_mode=pl.Buffered(1))


def _group_indicator(n, groups_padded, group):
    r = lax.broadcasted_iota(jnp.int32, (n, groups_padded), 0)
    c = lax.broadcasted_iota(jnp.int32, (n, groups_padded), 1)
    return (r // group == c).astype(F32)


def _head_rms(x, gain_tiled):
    n = x.shape[-1]
    g = _group_indicator(n, LANES, B_HEAD_DIM)
    ss = jnp.dot(x * x, g, preferred_element_type=F32, precision=HIGHEST)
    rs = lax.rsqrt(ss * (1.0 / B_HEAD_DIM) + RMS_EPS)
    rs_full = lax.dot_general(rs, g, _NT, preferred_element_type=F32, precision=HIGHEST)
    return x * rs_full * gain_tiled


def _norm_matmul_kernel(x_ref, g_ref, w_ref, o_ref, xn_ref):
    @pl.when(pl.program_id(1) == 0)
    def _():
        xn_ref[...] = _rms(x_ref[...], g_ref[...]).astype(BF16)

    o_ref[...] = jnp.dot(xn_ref[...], w_ref[...], preferred_element_type=F32).astype(o_ref.dtype)


def _norm_matmul(x, g, w, *, tm, tn, out_dtype=F32):
    m, k = x.shape
    n = w.shape[1]
    return pl.pallas_call(
        _norm_matmul_kernel,
        grid=(m // tm, n // tn),
        in_specs=[
            pl.BlockSpec((tm, k), lambda i, j: (i, 0)),
            pl.BlockSpec((1, k), lambda i, j: (0, 0)),
            pl.BlockSpec((k, tn), lambda i, j: (0, j)),
        ],
        out_specs=pl.BlockSpec((tm, tn), lambda i, j: (i, j)),
        out_shape=jax.ShapeDtypeStruct((m, n), out_dtype),
        scratch_shapes=[pltpu.VMEM((tm, k), BF16)],
        compiler_params=_params("parallel", "arbitrary"),
        name="norm_matmul",
    )(x, g, w)


def _matmul_res_kernel(a_ref, w_ref, r_ref, o_ref):
    o_ref[...] = r_ref[...] + jnp.dot(a_ref[...].astype(BF16), w_ref[...], preferred_element_type=F32)


def _matmul_res(a, w, res, *, tm):
    m, k = a.shape
    n = w.shape[1]
    return pl.pallas_call(
        _matmul_res_kernel,
        grid=(m // tm,),
        in_specs=[
            pl.BlockSpec((tm, k), lambda i: (i, 0)),
            _resident((k, n)),
            pl.BlockSpec((tm, n), lambda i: (i, 0)),
        ],
        out_specs=pl.BlockSpec((tm, n), lambda i: (i, 0)),
        out_shape=jax.ShapeDtypeStruct((m, n), F32),
        compiler_params=_params("parallel"),
        name="matmul_res",
    )(a, w, res)


def _ffn_kernel(x_ref, g_ref, wg_ref, wu_ref, wd_ref, o_ref, h_ref, act_ref, *, tf):
    x = x_ref[...]
    h_ref[...] = _rms(x, g_ref[...]).astype(BF16)
    for j in range(act_ref.shape[1] // tf):
        sl = slice(j * tf, (j + 1) * tf)
        a = jnp.dot(h_ref[...], wg_ref[:, sl], preferred_element_type=F32)
        b = jnp.dot(h_ref[...], wu_ref[:, sl], preferred_element_type=F32)
        act_ref[:, sl] = (jax.nn.silu(a) * b).astype(BF16)
    o_ref[...] = x + jnp.dot(act_ref[...], wd_ref[...], preferred_element_type=F32)


def _ffn(x, g, wg, wu, wd, *, tm, tf=256):
    m, d = x.shape
    dff = wg.shape[1]
    return pl.pallas_call(
        functools.partial(_ffn_kernel, tf=tf),
        grid=(m // tm,),
        in_specs=[
            pl.BlockSpec((tm, d), lambda i: (i, 0)),
            pl.BlockSpec((1, d), lambda i: (0, 0)),
            _resident((d, dff)),
            _resident((d, dff)),
            _resident((dff, d)),
        ],
        out_specs=pl.BlockSpec((tm, d), lambda i: (i, 0)),
        out_shape=jax.ShapeDtypeStruct((m, d), F32),
        scratch_shapes=[pltpu.VMEM((tm, d), BF16), pltpu.VMEM((tm, dff), BF16)],
        compiler_params=_params("parallel"),
        name="ffn",
    )(x, g, wg, wu, wd)


def _gla_chunk(q, z, v, lb, st, b_scr, k_scr, v_scr, *, c, u):
    sig = jax.nn.sigmoid(z)
    f = lb + (1.0 - lb) * sig
    logf = jnp.log(jnp.maximum(f, MIN_F))
    k = (1.0 - lb) * jax.nn.sigmoid(-z)
    tri = (lax.broadcasted_iota(jnp.int32, (c, c), 0) >= lax.broadcasted_iota(jnp.int32, (c, c), 1)).astype(F32)
    b = jnp.dot(tri, logf, preferred_element_type=F32, precision=HIGHEST)
    b2 = b * LOG2E
    b_scr[...] = b2
    k_scr[...] = k
    v_scr[...] = v
    vb = v.astype(BF16)
    n_sub = c // u
    rows = lax.broadcasted_iota(jnp.int32, (SUBLANES, 1), 0)

    o_state = lax.dot_general((q * jnp.exp(b)).astype(BF16), st.astype(BF16), _NT, preferred_element_type=F32)

    parts = []
    for i in range(n_sub):
        lo = u * i
        for j in range(u // SUBLANES):
            p0 = lo + j * SUBLANES
            qp = q[p0:p0 + SUBLANES]
            bp = b2[p0:p0 + SUBLANES]
            acc = jnp.zeros((SUBLANES, A_DV), F32)
            for s in range(lo, p0 + SUBLANES):
                d = bp - b_scr[s:s + 1, :]
                if s >= p0:
                    d = jnp.minimum(d, 0.0)
                a = jnp.sum(qp * k_scr[s:s + 1, :] * jnp.exp2(d), axis=-1, keepdims=True)
                if s >= p0:
                    a = jnp.where(rows >= s - p0, a, 0.0)
                acc = acc + a * v_scr[s:s + 1, :]
            parts.append(acc)
        if i > 0:
            r = b[lo - 1:lo, :]
            qt = (q[lo:lo + u] * jnp.exp(b[lo:lo + u] - r)).astype(BF16)
            kt = (k[:lo] * jnp.exp(r - b[:lo])).astype(BF16)
            att = lax.dot_general(qt, kt, _NT, preferred_element_type=F32)
            off = jnp.dot(att.astype(BF16), vb[:lo], preferred_element_type=F32)
            for j in range(u // SUBLANES):
                idx = (lo + j * SUBLANES) // SUBLANES
                parts[idx] = parts[idx] + off[j * SUBLANES:(j + 1) * SUBLANES]
    o = (parts[0] if len(parts) == 1 else jnp.concatenate(parts, axis=0)) + o_state

    bl = b[c - 1:c, :]
    kd = (k * jnp.exp(bl - b)).astype(BF16)
    st_new = st * jnp.exp(bl) + lax.dot_general(vb, kd, _TN, preferred_element_type=F32)
    return o, st_new


def _gla_kernel(*refs, layer, c, u, n_chunks, hps, has_s0):
    if has_s0:
        q_ref, z_ref, v_ref, gt_ref, lb_ref, og_ref, s0_ref, o_ref, so_ref, st_ref, b_scr, k_scr, v_scr = refs
    else:
        q_ref, z_ref, v_ref, gt_ref, lb_ref, og_ref, o_ref, so_ref, st_ref, b_scr, k_scr, v_scr = refs
        s0_ref = None
    i = pl.program_id(2)

    @pl.when(i == 0)
    def _():
        for hh in range(hps):
            st_ref[hh] = s0_ref[0, hh].T if has_s0 else jnp.zeros((A_DV, A_DK), F32)

    a = lb_ref[...]
    e = jnp.exp(a - jnp.max(a, axis=0, keepdims=True))
    p = e / jnp.sum(e, axis=0, keepdims=True)
    lb_all = jnp.sum(p[:layer + 1], axis=0, keepdims=True) - p[0:1]
    og = og_ref[...]

    def body(ci, carry):
        rs = pl.ds(pl.multiple_of(ci * c, c), c)
        for hh in range(hps):
            cs = slice(hh * LANES, (hh + 1) * LANES)
            o, st_new = _gla_chunk(q_ref[rs, cs], z_ref[rs, cs], v_ref[rs, cs], lb_all[:, cs], st_ref[hh],
                                   b_scr.at[hh], k_scr.at[hh], v_scr.at[hh], c=c, u=u)
            st_ref[hh] = st_new
            on = o * lax.rsqrt(jnp.mean(o * o, axis=-1, keepdims=True) + RMS_EPS) * og
            o_ref[rs, cs] = (on * jax.nn.silu(gt_ref[rs, cs])).astype(o_ref.dtype)
        return carry

    if n_chunks == 1:
        body(0, 0)
    else:
        lax.fori_loop(0, n_chunks, body, 0)

    @pl.when(i == pl.num_programs(2) - 1)
    def _():
        for hh in range(hps):
            so_ref[0, hh] = st_ref[hh].T


def _gla(h, a_lb, o_gain, s0, *, layer, batch, seq, c, u, t, hps, out_dtype):
    m = batch * seq
    nt = seq // t
    hb = A_HEADS // hps
    w = hps * LANES
    kd = A_HEADS * A_DK
    nblk = kd // w

    def col(off):
        return lambda b, hg, i: (b * nt + i, off * nblk + hg)

    in_specs = [
        pl.BlockSpec((t, w), col(0)),
        pl.BlockSpec((t, w), col(1)),
        pl.BlockSpec((t, w), col(2)),
        pl.BlockSpec((t, w), col(3)),
        pl.BlockSpec((a_lb.shape[0], w), lambda b, hg, i: (0, hg)),
        pl.BlockSpec((1, A_DV), lambda b, hg, i: (0, 0)),
    ]
    args = [h, h, h, h, a_lb, o_gain]
    if s0 is not None:
        in_specs.append(pl.BlockSpec((1, hps, A_DK, A_DV), lambda b, hg, i: (b, hg, 0, 0)))
        args.append(s0)
    kern = functools.partial(_gla_kernel, layer=layer, c=c, u=u, n_chunks=t // c, hps=hps, has_s0=s0 is not None)
    return pl.pallas_call(
        kern,
        grid=(batch, hb, nt),
        in_specs=in_specs,
        out_specs=[
            pl.BlockSpec((t, w), lambda b, hg, i: (b * nt + i, hg)),
            pl.BlockSpec((1, hps, A_DK, A_DV), lambda b, hg, i: (b, hg, 0, 0)),
        ],
        out_shape=[
            jax.ShapeDtypeStruct((m, kd), out_dtype),
            jax.ShapeDtypeStruct((batch, A_HEADS, A_DK, A_DV), F32),
        ],
        scratch_shapes=[
            pltpu.VMEM((hps, A_DV, A_DK), F32),
            pltpu.VMEM((hps, c, A_DK), F32),
            pltpu.VMEM((hps, c, A_DK), F32),
            pltpu.VMEM((hps, c, A_DV), F32),
        ],
        compiler_params=_params("parallel", "parallel", "arbitrary"),
        name="hgrn2_gla",
    )(*args)


def _kv_kernel(x_ref, g_ref, wkt_ref, wvt_ref, wft_ref, bf_ref, kg_ref, *out_refs, with_rows):
    kt_ref, vt_ref, lft_ref, kb_ref, vbt_ref = out_refs[:5]
    h = _rms(x_ref[...], g_ref[...]).astype(BF16)
    tm = h.shape[0]
    kt = lax.dot_general(wkt_ref[...], h, _NT, preferred_element_type=F32)
    k3 = kt.reshape(B_HEADS, B_HEAD_DIM, tm)
    ms = jnp.mean(k3 * k3, axis=1, keepdims=True)
    k3 = k3 * lax.rsqrt(ms + RMS_EPS) * kg_ref[...][None]
    vt = lax.dot_general(wvt_ref[...], h, _NT, preferred_element_type=F32)
    fz = lax.dot_general(wft_ref[...], h, _NT, preferred_element_type=F32) + bf_ref[...]
    krow = k3.reshape(B_HEADS * B_HEAD_DIM, tm).T
    kt_ref[0] = k3
    vt_ref[0] = vt.reshape(B_HEADS, B_HEAD_DIM, tm)
    lft_ref[0] = jnp.minimum(fz, 0.0) - jnp.log1p(jnp.exp(-jnp.abs(fz)))
    kb_ref[...] = krow.astype(BF16)
    vbt_ref[0] = vt.reshape(B_HEADS, B_HEAD_DIM, tm).astype(BF16)
    if with_rows:
        krow_ref, vrow_ref = out_refs[5:]
        krow_ref[...] = krow
        vrow_ref[...] = vt.T


def _shared_kv(x, g, wkt, wvt, wft, bf, kg, *, batch, seq, tm, with_rows):
    m, d = x.shape
    hd = B_HEADS * B_HEAD_DIM
    nt = seq // tm
    t_spec = pl.BlockSpec((1, B_HEADS, B_HEAD_DIM, tm), lambda b, i: (b, 0, 0, i))
    row_spec = pl.BlockSpec((tm, hd), lambda b, i: (b * nt + i, 0))
    t_shape = (batch, B_HEADS, B_HEAD_DIM, seq)
    out_specs = [t_spec, t_spec, pl.BlockSpec((1, B_HEADS, tm), lambda b, i: (b, 0, i)), row_spec, t_spec]
    out_shape = [
        jax.ShapeDtypeStruct(t_shape, F32),
        jax.ShapeDtypeStruct(t_shape, F32),
        jax.ShapeDtypeStruct((batch, B_HEADS, seq), F32),
        jax.ShapeDtypeStruct((m, hd), BF16),
        jax.ShapeDtypeStruct(t_shape, BF16),
    ]
    if with_rows:
        out_specs += [row_spec, row_spec]
        out_shape += [jax.ShapeDtypeStruct((m, hd), F32)] * 2
    return pl.pallas_call(
        functools.partial(_kv_kernel, with_rows=with_rows),
        grid=(batch, nt),
        in_specs=[
            pl.BlockSpec((tm, d), lambda b, i: (b * nt + i, 0)),
            pl.BlockSpec((1, d), lambda b, i: (0, 0)),
            _resident((hd, d)),
            _resident((hd, d)),
            _resident((B_HEADS, d)),
            pl.BlockSpec((B_HEADS, 1), lambda b, i: (0, 0)),
            pl.BlockSpec((B_HEAD_DIM, 1), lambda b, i: (0, 0)),
        ],
        out_specs=out_specs,
        out_shape=out_shape,
        compiler_params=_params("parallel", "parallel"),
        name="shared_kv",
    )(x, g, wkt, wvt, wft, bf, kg)


def _qproj_kernel(x_ref, g_ref, wt_ref, qg_ref, o_ref):
    h = _rms(x_ref[...], g_ref[...]).astype(BF16)
    tm = h.shape[0]
    qt = lax.dot_general(wt_ref[...], h, _NT, preferred_element_type=F32)
    q3 = qt.reshape(B_HEADS, B_HEAD_DIM, tm)
    ms = jnp.mean(q3 * q3, axis=1, keepdims=True)
    q3 = q3 * lax.rsqrt(ms + RMS_EPS) * qg_ref[...][None] * (B_HEAD_DIM ** -0.5)
    o_ref[...] = q3.reshape(B_HEADS * B_HEAD_DIM, tm).astype(BF16)


def _qproj_t(x, g, wt, qg, *, tm):
    m, d = x.shape
    hd = wt.shape[0]
    return pl.pallas_call(
        _qproj_kernel,
        grid=(m // tm,),
        in_specs=[
            pl.BlockSpec((tm, d), lambda i: (i, 0)),
            pl.BlockSpec((1, d), lambda i: (0, 0)),
            _resident((hd, d)),
            pl.BlockSpec((B_HEAD_DIM, 1), lambda i: (0, 0)),
        ],
        out_specs=pl.BlockSpec((hd, tm), lambda i: (0, i)),
        out_shape=jax.ShapeDtypeStruct((hd, m), BF16),
        compiler_params=_params("parallel"),
        name="qproj_t",
    )(x, g, wt, qg)


def _cumsum_prompt_kernel(x_ref, row_ref, col_ref, carry_scr):
    @pl.when(pl.program_id(1) == 0)
    def _():
        carry_scr[...] = jnp.zeros_like(carry_scr)

    x = x_ref[0]
    t = x.shape[1]
    upper = (lax.broadcasted_iota(jnp.int32, (t, t), 0) <= lax.broadcasted_iota(jnp.int32, (t, t), 1)).astype(F32)
    row = jnp.dot(x, upper, preferred_element_type=F32, precision=HIGHEST) + carry_scr[...]
    row_ref[0] = row
    col_ref[0] = row.T
    carry_scr[...] = row[:, t - 1:t]


def _cumsum_prompt(lft, *, t=256):
    b, h, l = lft.shape
    return pl.pallas_call(
        _cumsum_prompt_kernel,
        grid=(b, l // t),
        in_specs=[pl.BlockSpec((1, h, t), lambda i, j: (i, 0, j))],
        out_specs=[
            pl.BlockSpec((1, h, t), lambda i, j: (i, 0, j)),
            pl.BlockSpec((1, t, h), lambda i, j: (i, j, 0)),
        ],
        out_shape=[jax.ShapeDtypeStruct((b, h, l), F32), jax.ShapeDtypeStruct((b, l, h), F32)],
        scratch_shapes=[pltpu.VMEM((h, 1), F32)],
        compiler_params=_params("parallel", "arbitrary"),
        name="cumsum_prompt",
    )(lft)


def _fox_prefill_kernel(qt_ref, k_ref, vt_ref, ccol_ref, crow_ref, o_ref, m_scr, l_scr, acc_scr, *, tq, tk):
    pair = pl.program_id(1)
    qi = pl.program_id(2)
    row_head = lax.broadcasted_iota(jnp.int32, (2 * B_HEAD_DIM, 1), 0) // B_HEAD_DIM
    qt = qt_ref[...]
    qm = [jnp.where(row_head == hh, qt, jnp.zeros_like(qt)) for hh in range(2)]
    hl = lax.broadcasted_iota(jnp.int32, (1, B_HEADS), 1)
    nsub = tq // tk

    m_scr[...] = jnp.full_like(m_scr, -jnp.inf)
    l_scr[...] = jnp.zeros_like(l_scr)
    acc_scr[...] = jnp.zeros_like(acc_scr)

    def step(kb, diag):
        ks = pl.ds(pl.multiple_of(kb * tk, tk), tk)
        k = k_ref[ks, :]
        ccol = ccol_ref[0, ks, :]
        for hh in range(2):
            ck = jnp.sum(jnp.where(hl == 2 * pair + hh, ccol, 0.0), axis=-1, keepdims=True)
            y = jnp.dot(k, qm[hh], preferred_element_type=F32) - ck
            if diag is not None:
                r = lax.broadcasted_iota(jnp.int32, (tk, tq), 0) + diag
                c = lax.broadcasted_iota(jnp.int32, (tk, tq), 1)
                y = jnp.where(r <= c, y, MASK_VALUE)
            cq = crow_ref[0, 0, hh:hh + 1, :]
            m_old = m_scr[hh]
            m_new = jnp.maximum(m_old, jnp.max(y, axis=0, keepdims=True) + cq)
            alpha = jnp.exp(m_old - m_new)
            p = jnp.exp(y - (m_new - cq))
            l_scr[hh] = alpha * l_scr[hh] + jnp.sum(p, axis=0, keepdims=True)
            vt = vt_ref[0, 0, hh * B_HEAD_DIM:(hh + 1) * B_HEAD_DIM, ks]
            acc_scr[hh] = alpha * acc_scr[hh] + jnp.dot(vt, p.astype(BF16), preferred_element_type=F32)
            m_scr[hh] = m_new

    def body(kb, carry):
        step(kb, None)
        return carry

    lax.fori_loop(0, qi * nsub, body, 0)
    for j in range(nsub):
        step(qi * nsub + j, j * tk)
    ot = jnp.concatenate([acc_scr[0] / l_scr[0], acc_scr[1] / l_scr[1]], axis=0)
    o_ref[...] = ot.T.astype(o_ref.dtype)


def _fox_prefill(qt, kb, vbt, ccol, crow, *, batch, seq, tq, tk):
    m = batch * seq
    nq = seq // tq
    npair = B_HEADS // 2
    pw = 2 * B_HEAD_DIM
    vbt4 = vbt.reshape(batch, npair, pw, seq)
    crow4 = crow.reshape(batch, npair, 2, seq)
    return pl.pallas_call(
        functools.partial(_fox_prefill_kernel, tq=tq, tk=tk),
        grid=(batch, npair, nq),
        in_specs=[
            pl.BlockSpec((pw, tq), lambda b, p, i: (p, b * nq + i)),
            pl.BlockSpec((seq, pw), lambda b, p, i: (b, p)),
            pl.BlockSpec((1, 1, pw, seq), lambda b, p, i: (b, p, 0, 0)),
            pl.BlockSpec((1, seq, B_HEADS), lambda b, p, i: (b, 0, 0)),
            pl.BlockSpec((1, 1, 2, tq), lambda b, p, i: (b, p, 0, i)),
        ],
        out_specs=pl.BlockSpec((tq, pw), lambda b, p, i: (b * nq + i, p)),
        out_shape=jax.ShapeDtypeStruct((m, B_HEADS * B_HEAD_DIM), BF16),
        scratch_shapes=[
            pltpu.VMEM((2, 1, tq), F32),
            pltpu.VMEM((2, 1, tq), F32),
            pltpu.VMEM((2, B_HEAD_DIM, tq), F32),
        ],
        compiler_params=_params("parallel", "parallel", "arbitrary"),
        name="fox_prefill",
    )(qt, kb, vbt4, ccol, crow4)


def _cumsum_sample_kernel(*refs, n_steps, ppb):
    page_refs = refs[1:1 + ppb]
    new_ref, past_ref, newrow_ref, carry_scr = refs[1 + ppb:]
    j = pl.program_id(1)
    h = carry_scr.shape[0]

    @pl.when(j == 0)
    def _():
        carry_scr[...] = jnp.zeros_like(carry_scr)

    def local_cumsum(x):
        t = x.shape[1]
        upper = (lax.broadcasted_iota(jnp.int32, (t, PAGE_SIZE), 0)
                 <= lax.broadcasted_iota(jnp.int32, (t, PAGE_SIZE), 1)).astype(F32)
        return jnp.dot(x, upper, preferred_element_type=F32, precision=HIGHEST)

    loc = local_cumsum(jnp.concatenate([r[0] for r in page_refs], axis=0))
    carry = carry_scr[...]
    for jj in range(ppb):
        row = loc[jj * h:(jj + 1) * h] + carry
        past_ref[0, :, jj * PAGE_SIZE:(jj + 1) * PAGE_SIZE] = row
        carry = row[:, PAGE_SIZE - 1:PAGE_SIZE]
    carry_scr[...] = carry

    @pl.when(j == n_steps - 1)
    def _():
        newrow_ref[0] = local_cumsum(new_ref[0]) + carry


def _cumsum_sample(page_table_flat, cache_lft, lft_new, *, batch, n_pages, dec_seq, ppb=16):
    h = cache_lft.shape[1]
    n_steps = n_pages // ppb

    def page(jj):
        return lambda b, j, pt: (pt[b * n_pages + j * ppb + jj], 0, 0)

    grid_spec = pltpu.PrefetchScalarGridSpec(
        num_scalar_prefetch=1,
        grid=(batch, n_steps),
        in_specs=[pl.BlockSpec((1, h, PAGE_SIZE), page(jj)) for jj in range(ppb)]
        + [pl.BlockSpec((1, h, dec_seq), lambda b, j, pt: (b, 0, 0))],
        out_specs=[
            pl.BlockSpec((1, h, ppb * PAGE_SIZE), lambda b, j, pt: (b, 0, j)),
            pl.BlockSpec((1, h, PAGE_SIZE), lambda b, j, pt: (b, 0, 0)),
        ],
        scratch_shapes=[pltpu.VMEM((h, 1), F32)],
    )
    return pl.pallas_call(
        functools.partial(_cumsum_sample_kernel, n_steps=n_steps, ppb=ppb),
        grid_spec=grid_spec,
        out_shape=[
            jax.ShapeDtypeStruct((batch, h, n_pages * PAGE_SIZE), F32),
            jax.ShapeDtypeStruct((batch, h, PAGE_SIZE), F32),
        ],
        compiler_params=_params("parallel", "arbitrary"),
        name="cumsum_sample",
    )(page_table_flat, *([cache_lft] * ppb), lft_new)


def _fox_decode_kernel(*refs, ppb, dec_seq, n_steps):
    hq_ref, qg_ref = refs[1:3]
    k_refs = refs[3:3 + ppb]
    v_refs = refs[3 + ppb:3 + 2 * ppb]
    ck_ref, ckn_ref, cq_ref, kn_ref, vn_ref, o_ref = refs[3 + 2 * ppb:9 + 2 * ppb]
    qbd_scr, m_scr, l_scr, acc_scr, bias_scr, knp_scr, vnp_scr = refs[9 + 2 * ppb:]
    s_id = pl.program_id(1)
    hd = B_HEADS * B_HEAD_DIM
    nrow = B_HEADS * dec_seq
    lane_head = lax.broadcasted_iota(jnp.int32, (1, hd), 1) // B_HEAD_DIM

    @pl.when(s_id == 0)
    def _():
        qn = _head_rms(hq_ref[...], qg_ref[...]) * (B_HEAD_DIM ** -0.5)
        for h2 in range(B_HEADS // 2):
            a = jnp.where(lane_head == 2 * h2, qn, 0.0)
            b = jnp.where(lane_head == 2 * h2 + 1, qn, 0.0)
            qbd_scr[2 * h2 * dec_seq:(2 * h2 + 2) * dec_seq, :] = jnp.concatenate([a, b], axis=0).astype(BF16)
        m_scr[...] = jnp.full_like(m_scr, -jnp.inf)
        l_scr[...] = jnp.zeros_like(l_scr)
        acc_scr[...] = jnp.zeros_like(acc_scr)
        knp_scr[...] = jnp.zeros_like(knp_scr)
        vnp_scr[...] = jnp.zeros_like(vnp_scr)
        knp_scr[:, 0:dec_seq] = kn_ref[0]
        vnp_scr[:, 0:dec_seq] = vn_ref[0]

    cq = cq_ref[0]

    def attend(kt_tiles, vt_tiles, ck_rows, mask):
        n = len(kt_tiles)
        qbd = qbd_scr[...]
        s = jnp.concatenate([jnp.dot(qbd, kt, preferred_element_type=F32) for kt in kt_tiles], axis=1)
        for h in range(B_HEADS):
            rs = slice(h * dec_seq, (h + 1) * dec_seq)
            bias_scr[rs, 0:n * PAGE_SIZE] = cq[rs] - ck_rows[h:h + 1, :]
        s = s + bias_scr[:, 0:n * PAGE_SIZE]
        if mask is not None:
            s = jnp.where(mask, s, MASK_VALUE)
        m_old = m_scr[...]
        m_new = jnp.maximum(m_old, jnp.max(s, axis=-1, keepdims=True))
        alpha = jnp.exp(m_old - m_new)
        p = jnp.exp(s - m_new)
        l_scr[...] = alpha * l_scr[...] + jnp.sum(p, axis=-1, keepdims=True)
        pv = None
        for jj in range(n):
            t = lax.dot_general(p[:, jj * PAGE_SIZE:(jj + 1) * PAGE_SIZE].astype(BF16), vt_tiles[jj], _NT,
                                preferred_element_type=F32)
            pv = t if pv is None else pv + t
        acc_scr[...] = alpha * acc_scr[...] + pv
        m_scr[...] = m_new

    attend([r[0].reshape(hd, PAGE_SIZE).astype(BF16) for r in k_refs],
           [r[0].reshape(hd, PAGE_SIZE).astype(BF16) for r in v_refs], ck_ref[0], None)

    @pl.when(s_id == n_steps - 1)
    def _():
        r = lax.broadcasted_iota(jnp.int32, (nrow, PAGE_SIZE), 0) % dec_seq
        c = lax.broadcasted_iota(jnp.int32, (nrow, PAGE_SIZE), 1)
        attend([knp_scr[...].astype(BF16)], [vnp_scr[...].astype(BF16)], ckn_ref[0], c <= r)
        o = acc_scr[...] / l_scr[...]
        out = jnp.zeros((dec_seq, hd), F32)
        for h in range(B_HEADS):
            out = out + jnp.where(lane_head == h, o[h * dec_seq:(h + 1) * dec_seq, :], 0.0)
        o_ref[...] = out.astype(o_ref.dtype)


def _fox_decode(page_table_flat, hq, qg_full, cache_kt, cache_vt, ck_past, ck_new, cq_col, kt_new, vt_new,
                *, batch, n_pages, dec_seq, ppb):
    hd = B_HEADS * B_HEAD_DIM
    n_steps = n_pages // ppb
    nrow = B_HEADS * dec_seq

    def page(jj):
        return lambda b, s, pt: (pt[b * n_pages + s * ppb + jj], 0, 0, 0)

    page_block = (1, B_HEADS, B_HEAD_DIM, PAGE_SIZE)
    in_specs = [
        pl.BlockSpec((dec_seq, hd), lambda b, s, pt: (b, 0)),
        pl.BlockSpec((1, hd), lambda b, s, pt: (0, 0)),
    ]
    in_specs += [pl.BlockSpec(page_block, page(jj)) for jj in range(ppb)]
    in_specs += [pl.BlockSpec(page_block, page(jj)) for jj in range(ppb)]
    in_specs += [
        pl.BlockSpec((1, B_HEADS, ppb * PAGE_SIZE), lambda b, s, pt: (b, 0, s)),
        pl.BlockSpec((1, B_HEADS, PAGE_SIZE), lambda b, s, pt: (b, 0, 0)),
        pl.BlockSpec((1, nrow, 1), lambda b, s, pt: (b, 0, 0)),
        pl.BlockSpec((1, hd, dec_seq), lambda b, s, pt: (b, 0, 0)),
        pl.BlockSpec((1, hd, dec_seq), lambda b, s, pt: (b, 0, 0)),
    ]
    grid_spec = pltpu.PrefetchScalarGridSpec(
        num_scalar_prefetch=1,
        grid=(batch, n_steps),
        in_specs=in_specs,
        out_specs=pl.BlockSpec((dec_seq, hd), lambda b, s, pt: (b, 0)),
        scratch_shapes=[
            pltpu.VMEM((nrow, hd), BF16),
            pltpu.VMEM((nrow, 1), F32),
            pltpu.VMEM((nrow, 1), F32),
            pltpu.VMEM((nrow, hd), F32),
            pltpu.VMEM((nrow, ppb * PAGE_SIZE), F32),
            pltpu.VMEM((hd, PAGE_SIZE), F32),
            pltpu.VMEM((hd, PAGE_SIZE), F32),
        ],
    )
    args = ([page_table_flat, hq, qg_full] + [cache_kt] * ppb + [cache_vt] * ppb
            + [ck_past, ck_new, cq_col, kt_new, vt_new])
    return pl.pallas_call(
        functools.partial(_fox_decode_kernel, ppb=ppb, dec_seq=dec_seq, n_steps=n_steps),
        grid_spec=grid_spec,
        out_shape=jax.ShapeDtypeStruct((batch * dec_seq, hd), F32),
        compiler_params=_params("parallel", "arbitrary"),
        name="fox_decode",
    )(*args)


def _trunk(x, batch, seq, states, w, attention, *, tm, kv_tm, kv_rows, gla_cfg):
    n_a = w["a_w_in"].shape[0]
    depth = w["wg"].shape[0]
    new_states = []
    kv = ctx = None
    for layer in range(depth):
        if layer < n_a:
            h = _norm_matmul(x, w["a_norm"][layer:layer + 1], w["a_w_in"][layer], tm=tm, tn=1024)
            s0 = None if states is None else states[layer]
            og, s_new = _gla(h, w["a_lb"], w["a_onorm"][layer:layer + 1], s0, layer=layer, batch=batch, seq=seq, **gla_cfg)
            new_states.append(s_new)
            x = _matmul_res(og, w["a_w_out"][layer], x, tm=tm)
        else:
            j = layer - n_a
            if j == 0:
                kv = _shared_kv(x, w["kv_norm"], w["wkt"], w["wvt"], w["wft"], w["bf"], w["kg"],
                                batch=kv_rows[0], seq=kv_rows[1], tm=kv_tm, with_rows=attention.with_rows)
                ctx = attention.prepare(kv)
            o = attention.run(j, x, w, ctx, kv)
            x = _matmul_res(o, w["b_wo"][j], x, tm=tm)
        x = _ffn(x, w["ffn_norm"][layer:layer + 1], w["wg"][layer], w["wu"][layer], w["wd"][layer], tm=min(tm, 512))
    return x, jnp.stack(new_states), kv


class _PromptAttention:
    with_rows = False

    def __init__(self, batch, seq, tq, tk):
        self.batch, self.seq, self.tq, self.tk = batch, seq, tq, tk

    def prepare(self, kv):
        return _cumsum_prompt(kv[2])

    def run(self, j, x, w, ctx, kv):
        crow, ccol = ctx
        qt = _qproj_t(x, w["b_norm"][j:j + 1], w["b_wq_t"][j], w["qg_col"][j], tm=512)
        return _fox_prefill(qt, kv[3], kv[4], ccol, crow, batch=self.batch, seq=self.seq, tq=self.tq, tk=self.tk)


class _SampleAttention:
    with_rows = True

    def __init__(self, batch, dec_seq, n_pages, pt_flat, cache_kt, cache_vt, cache_lft, ppb):
        self.batch, self.dec_seq, self.n_pages, self.pt = batch, dec_seq, n_pages, pt_flat
        self.cache_kt, self.cache_vt, self.cache_lft, self.ppb = cache_kt, cache_vt, cache_lft, ppb

    def _per_batch(self, t):
        rows = t.shape[0]
        return t.reshape(rows, self.batch, self.dec_seq).transpose(1, 0, 2)

    def prepare(self, kv):
        hd = B_HEADS * B_HEAD_DIM
        lft_new = self._per_batch(kv[2][0])
        ck_past, ck_new = _cumsum_sample(self.pt, self.cache_lft, lft_new, batch=self.batch, n_pages=self.n_pages,
                                         dec_seq=self.dec_seq)
        cq_col = ck_new[:, :, :self.dec_seq].reshape(self.batch, B_HEADS * self.dec_seq, 1)
        kt_new = self._per_batch(kv[0].reshape(hd, -1))
        vt_new = self._per_batch(kv[1].reshape(hd, -1))
        return ck_past, ck_new, cq_col, kt_new, vt_new

    def run(self, j, x, w, ctx, kv):
        ck_past, ck_new, cq_col, kt_new, vt_new = ctx
        hq = _norm_matmul(x, w["b_norm"][j:j + 1], w["b_wq"][j], tm=x.shape[0], tn=1024)
        return _fox_decode(self.pt, hq, w["qg_full"][j:j + 1], self.cache_kt, self.cache_vt, ck_past, ck_new, cq_col,
                           kt_new, vt_new, batch=self.batch, n_pages=self.n_pages, dec_seq=self.dec_seq, ppb=self.ppb)


def kernel(x_prompt, x_sample, state_hgrn, cache_k, cache_v, cache_logf, page_table, a_norm, a_w_in, a_lb, a_onorm, a_w_out, kv_norm, w_kvf, b_f, k_norm, b_norm, b_wq, q_norm, b_wo, ffn_norm, w_gate_up, w_down):
    batch, seq, d = x_prompt.shape
    dec_batch, dec_seq, _ = x_sample.shape
    n_pages = page_table.shape[1]
    hd = B_HEADS * B_HEAD_DIM
    dff = w_down.shape[1]
    assert seq % A_CHUNK == 0 and dec_seq <= A_CHUNK

    w_kvf_t = w_kvf.T.astype(BF16)
    w = dict(
        a_norm=a_norm, a_w_in=a_w_in.astype(BF16), a_lb=a_lb.astype(F32), a_onorm=a_onorm,
        a_w_out=a_w_out.astype(BF16), kv_norm=kv_norm.reshape(1, d),
        wkt=w_kvf_t[:hd], wvt=w_kvf_t[hd:2 * hd], wft=w_kvf_t[2 * hd:],
        bf=b_f.astype(F32).reshape(B_HEADS, 1), kg=k_norm.astype(F32).reshape(B_HEAD_DIM, 1),
        b_norm=b_norm, b_wq=b_wq.astype(BF16), b_wo=b_wo.astype(BF16), ffn_norm=ffn_norm,
        wg=w_gate_up[:, :, :dff].astype(BF16), wu=w_gate_up[:, :, dff:].astype(BF16), wd=w_down.astype(BF16),
    )
    w["b_wq_t"] = jnp.swapaxes(b_wq, 1, 2).astype(BF16)
    w["qg_col"] = q_norm.astype(F32)[:, :, None]
    w["qg_full"] = jnp.tile(q_norm, (1, B_HEADS))

    prompt_att = _PromptAttention(batch, seq, tq=512, tk=512)
    y_p, hgrn_p, kv_p = _trunk(
        x_prompt.reshape(batch * seq, d), batch, seq, None, w, prompt_att, tm=1024, kv_tm=512, kv_rows=(batch, seq),
        gla_cfg=dict(c=A_CHUNK, u=A_SUB, t=256, hps=8, out_dtype=BF16))
    k_p = kv_p[0].transpose(0, 3, 1, 2)
    v_p = kv_p[1].transpose(0, 3, 1, 2)
    lf_p = kv_p[2].transpose(0, 2, 1)

    m_s = dec_batch * dec_seq
    pt_flat = page_table.reshape(-1).astype(jnp.int32)
    sample_att = _SampleAttention(dec_batch, dec_seq, n_pages, pt_flat,
                                  cache_k.transpose(0, 2, 3, 1), cache_v.transpose(0, 2, 3, 1),
                                  cache_logf.transpose(0, 2, 1), ppb=8)
    y_s, hgrn_s, kv_s = _trunk(
        x_sample.reshape(m_s, d), dec_batch, dec_seq, state_hgrn, w, sample_att, tm=m_s, kv_tm=m_s, kv_rows=(1, m_s),
        gla_cfg=dict(c=dec_seq, u=min(A_SUB, dec_seq), t=dec_seq, hps=A_HEADS, out_dtype=F32))
    k_s = kv_s[5].reshape(dec_batch, dec_seq, B_HEADS, B_HEAD_DIM)
    v_s = kv_s[6].reshape(dec_batch, dec_seq, B_HEADS, B_HEAD_DIM)
    lf_s = kv_s[2][0].T.reshape(dec_batch, dec_seq, B_HEADS)

    return (y_p.reshape(batch, seq, d), y_s.reshape(dec_batch, dec_seq, d), hgrn_p, k_p, v_p, lf_p,
            hgrn_s, k_s, v_s, lf_s)
```

```python
import functools

import jax
import jax.numpy as jnp
from jax import lax
from jax.experimental import pallas as pl
from jax.experimental.pallas import tpu as pltpu

F32 = jnp.float32
BF16 = jnp.bfloat16
HIGHEST = lax.Precision.HIGHEST

D_MODEL = 1024
A_HEADS = 8
A_DK = 128
A_DV = 128
A_CHUNK = 64
A_SUB = 16
B_HEADS = 16
B_HEAD_DIM = 64
PAGE_SIZE = 128
RMS_EPS = 1e-6
MASK_VALUE = -1e30
MIN_F = 1e-30
LOG2E = 1.4426950408889634
LANES = 128
SUBLANES = 8
VMEM_LIMIT = 56 * 1024 * 1024

_NT = (((1,), (1,)), ((), ()))
_TN = (((0,), (0,)), ((), ()))


def _params(*sem):
    return pltpu.CompilerParams(dimension_semantics=sem, vmem_limit_bytes=VMEM_LIMIT)


def _rms(x, g):
    return x * lax.rsqrt(jnp.mean(x * x, axis=-1, keepdims=True) + RMS_EPS) * g


def _tri_matmul(x, tri, *, tri_on_left):
    hi = x.astype(BF16)
    r1 = x - hi.astype(F32)
    mid = r1.astype(BF16)
    lo = (r1 - mid.astype(F32)).astype(BF16)
    t = tri.astype(BF16)
    out = None
    for piece in (hi, mid, lo):
        d = jnp.dot(t, piece, preferred_element_type=F32) if tri_on_left else jnp.dot(piece, t, preferred_element_type=F32)
        out = d if out is None else out + d
    return out


def _resident(shape):
    return pl.BlockSpec(shape, lambda *_: (0,) * len(shape), pipeline_mode=pl.Buffered(1))


def _resident_layer(w3, layer, col_block=0, cols=None):
    _, r, c = w3.shape
    return pl.BlockSpec((None, r, cols or c), lambda *_: (layer, 0, col_block), pipeline_mode=pl.Buffered(1))


def _group_indicator(n, groups_padded, group):
    r = lax.broadcasted_iota(jnp.int32, (n, groups_padded), 0)
    c = lax.broadcasted_iota(jnp.int32, (n, groups_padded), 1)
    return (r // group == c).astype(F32)


def _head_rms(x, gain_tiled):
    n = x.shape[-1]
    g = _group_indicator(n, LANES, B_HEAD_DIM)
    ss = jnp.dot(x * x, g, preferred_element_type=F32, precision=HIGHEST)
    rs = lax.rsqrt(ss * (1.0 / B_HEAD_DIM) + RMS_EPS)
    rs_full = lax.dot_general(rs, g, _NT, preferred_element_type=F32, precision=HIGHEST)
    return x * rs_full * gain_tiled


def _norm_matmul_kernel(x_ref, g_ref, w_ref, o_ref, xn_ref):
    @pl.when(pl.program_id(1) == 0)
    def _():
        xn_ref[...] = _rms(x_ref[...], g_ref[...]).astype(BF16)

    o_ref[...] = jnp.dot(xn_ref[...], w_ref[...], preferred_element_type=F32).astype(o_ref.dtype)


def _norm_matmul(x, g, w3, layer, *, tm, tn, out_dtype=F32):
    m, k = x.shape
    n = w3.shape[2]
    return pl.pallas_call(
        _norm_matmul_kernel,
        grid=(m // tm, n // tn),
        in_specs=[
            pl.BlockSpec((tm, k), lambda i, j: (i, 0)),
            pl.BlockSpec((1, k), lambda i, j: (0, 0)),
            pl.BlockSpec((None, k, tn), lambda i, j: (layer, 0, j)),
        ],
        out_specs=pl.BlockSpec((tm, tn), lambda i, j: (i, j)),
        out_shape=jax.ShapeDtypeStruct((m, n), out_dtype),
        scratch_shapes=[pltpu.VMEM((tm, k), BF16)],
        compiler_params=_params("parallel", "arbitrary"),
        name="norm_matmul",
    )(x, g, w3)


def _matmul_res_kernel(a_ref, w_ref, r_ref, o_ref):
    o_ref[...] = r_ref[...] + jnp.dot(a_ref[...].astype(BF16), w_ref[...], preferred_element_type=F32)


def _matmul_res(a, w3, layer, res, *, tm):
    m, k = a.shape
    n = w3.shape[2]
    return pl.pallas_call(
        _matmul_res_kernel,
        grid=(m // tm,),
        in_specs=[
            pl.BlockSpec((tm, k), lambda i: (i, 0)),
            _resident_layer(w3, layer),
            pl.BlockSpec((tm, n), lambda i: (i, 0)),
        ],
        out_specs=pl.BlockSpec((tm, n), lambda i: (i, 0)),
        out_shape=jax.ShapeDtypeStruct((m, n), F32),
        compiler_params=_params("parallel"),
        name="matmul_res",
    )(a, w3, res)


def _ffn_kernel(x_ref, g_ref, wg_ref, wu_ref, wd_ref, o_ref, h_ref, act_ref, *, tf):
    x = x_ref[...]
    h_ref[...] = _rms(x, g_ref[...]).astype(BF16)
    for j in range(act_ref.shape[1] // tf):
        sl = slice(j * tf, (j + 1) * tf)
        a = jnp.dot(h_ref[...], wg_ref[:, sl], preferred_element_type=F32)
        b = jnp.dot(h_ref[...], wu_ref[:, sl], preferred_element_type=F32)
        act_ref[:, sl] = (jax.nn.silu(a) * b).astype(BF16)
    o_ref[...] = x + jnp.dot(act_ref[...], wd_ref[...], preferred_element_type=F32)


def _ffn(x, g, wgu3, wd3, layer, *, tm, tf=256):
    m, d = x.shape
    dff = wd3.shape[1]
    return pl.pallas_call(
        functools.partial(_ffn_kernel, tf=tf),
        grid=(m // tm,),
        in_specs=[
            pl.BlockSpec((tm, d), lambda i: (i, 0)),
            pl.BlockSpec((1, d), lambda i: (0, 0)),
            _resident_layer(wgu3, layer, 0, dff),
            _resident_layer(wgu3, layer, 1, dff),
            _resident_layer(wd3, layer),
        ],
        out_specs=pl.BlockSpec((tm, d), lambda i: (i, 0)),
        out_shape=jax.ShapeDtypeStruct((m, d), F32),
        scratch_shapes=[pltpu.VMEM((tm, d), BF16), pltpu.VMEM((tm, dff), BF16)],
        compiler_params=_params("parallel"),
        name="ffn",
    )(x, g, wgu3, wgu3, wd3)


def _gla_gates(z, lb, c):
    sig = jax.nn.sigmoid(z)
    f = lb + (1.0 - lb) * sig
    logf = jnp.log(jnp.maximum(f, MIN_F))
    k = (1.0 - lb) * jax.nn.sigmoid(-z)
    tri = lax.broadcasted_iota(jnp.int32, (c, c), 0) >= lax.broadcasted_iota(jnp.int32, (c, c), 1)
    return k, _tri_matmul(logf, tri, tri_on_left=True)


def _gla_diag(q, b2, b_scr, k_scr, v_scr, *, c, u):
    rows = lax.broadcasted_iota(jnp.int32, (SUBLANES, 1), 0)
    parts = []
    for i in range(c // u):
        lo = u * i
        for j in range(u // SUBLANES):
            p0 = lo + j * SUBLANES
            qp = q[p0:p0 + SUBLANES]
            bp = b2[p0:p0 + SUBLANES]
            acc = jnp.zeros((SUBLANES, A_DV), F32)
            for s in range(lo, p0 + SUBLANES):
                d = bp - b_scr[s:s + 1, :]
                if s >= p0:
                    d = jnp.minimum(d, 0.0)
                a = jnp.sum(qp * k_scr[s:s + 1, :] * jnp.exp2(d), axis=-1, keepdims=True)
                if s >= p0:
                    a = jnp.where(rows >= s - p0, a, 0.0)
                acc = acc + a * v_scr[s:s + 1, :]
            parts.append(acc)
    return parts


def _gla_mxu(q, k, b, v, st, parts, *, c, u):
    vb = v.astype(BF16)
    o_state = lax.dot_general((q * jnp.exp(b)).astype(BF16), st.astype(BF16), _NT, preferred_element_type=F32)
    parts = list(parts)
    for i in range(1, c // u):
        lo = u * i
        r = b[lo - 1:lo, :]
        qt = (q[lo:lo + u] * jnp.exp(b[lo:lo + u] - r)).astype(BF16)
        kt = (k[:lo] * jnp.exp(r - b[:lo])).astype(BF16)
        att = lax.dot_general(qt, kt, _NT, preferred_element_type=F32)
        off = jnp.dot(att.astype(BF16), vb[:lo], preferred_element_type=F32)
        for j in range(u // SUBLANES):
            idx = (lo + j * SUBLANES) // SUBLANES
            parts[idx] = parts[idx] + off[j * SUBLANES:(j + 1) * SUBLANES]
    o = (parts[0] if len(parts) == 1 else jnp.concatenate(parts, axis=0)) + o_state
    bl = b[c - 1:c, :]
    kd = (k * jnp.exp(bl - b)).astype(BF16)
    st_new = st * jnp.exp(bl) + lax.dot_general(vb, kd, _TN, preferred_element_type=F32)
    return o, st_new


def _gla_kernel(*refs, layer, c, u, n_chunks, hps, has_s0):
    if has_s0:
        q_ref, z_ref, v_ref, gt_ref, lb_ref, og_ref, s0_ref, o_ref, so_ref, st_ref, b_scr, k_scr, v_scr = refs
    else:
        q_ref, z_ref, v_ref, gt_ref, lb_ref, og_ref, o_ref, so_ref, st_ref, b_scr, k_scr, v_scr = refs
        s0_ref = None
    i = pl.program_id(2)

    @pl.when(i == 0)
    def _():
        for hh in range(hps):
            st_ref[hh] = s0_ref[0, hh].T if has_s0 else jnp.zeros((A_DV, A_DK), F32)

    a = lb_ref[...]
    e = jnp.exp(a - jnp.max(a, axis=0, keepdims=True))
    p = e / jnp.sum(e, axis=0, keepdims=True)
    lb_all = jnp.sum(p[:layer + 1], axis=0, keepdims=True) - p[0:1]
    og = og_ref[...]

    def body(ci, carry):
        rs = pl.ds(pl.multiple_of(ci * c, c), c)
        cols = [slice(hh * LANES, (hh + 1) * LANES) for hh in range(hps)]
        kb = [_gla_gates(z_ref[rs, cs], lb_all[:, cs], c) for cs in cols]
        for hh, cs in enumerate(cols):
            b_scr[hh] = kb[hh][1] * LOG2E
            k_scr[hh] = kb[hh][0]
            v_scr[hh] = v_ref[rs, cs]
        parts = [_gla_diag(q_ref[rs, cs], b_scr[hh], b_scr.at[hh], k_scr.at[hh], v_scr.at[hh], c=c, u=u)
                 for hh, cs in enumerate(cols)]
        for hh, cs in enumerate(cols):
            o, st_new = _gla_mxu(q_ref[rs, cs], kb[hh][0], kb[hh][1], v_ref[rs, cs], st_ref[hh], parts[hh], c=c, u=u)
            st_ref[hh] = st_new
            on = o * lax.rsqrt(jnp.mean(o * o, axis=-1, keepdims=True) + RMS_EPS) * og
            o_ref[rs, cs] = (on * jax.nn.silu(gt_ref[rs, cs])).astype(o_ref.dtype)
        return carry

    if n_chunks == 1:
        body(0, 0)
    else:
        lax.fori_loop(0, n_chunks, body, 0)

    @pl.when(i == pl.num_programs(2) - 1)
    def _():
        for hh in range(hps):
            so_ref[0, hh] = st_ref[hh].T


def _gla(h, a_lb, o_gain, s0, *, layer, batch, seq, c, u, t, hps, out_dtype):
    m = batch * seq
    nt = seq // t
    hb = A_HEADS // hps
    w = hps * LANES
    kd = A_HEADS * A_DK
    nblk = kd // w

    def col(off):
        return lambda b, hg, i: (b * nt + i, off * nblk + hg)

    in_specs = [
        pl.BlockSpec((t, w), col(0)),
        pl.BlockSpec((t, w), col(1)),
        pl.BlockSpec((t, w), col(2)),
        pl.BlockSpec((t, w), col(3)),
        pl.BlockSpec((a_lb.shape[0], w), lambda b, hg, i: (0, hg)),
        pl.BlockSpec((1, A_DV), lambda b, hg, i: (0, 0)),
    ]
    args = [h, h, h, h, a_lb, o_gain]
    if s0 is not None:
        in_specs.append(pl.BlockSpec((1, hps, A_DK, A_DV), lambda b, hg, i: (b, hg, 0, 0)))
        args.append(s0)
    kern = functools.partial(_gla_kernel, layer=layer, c=c, u=u, n_chunks=t // c, hps=hps, has_s0=s0 is not None)
    return pl.pallas_call(
        kern,
        grid=(batch, hb, nt),
        in_specs=in_specs,
        out_specs=[
            pl.BlockSpec((t, w), lambda b, hg, i: (b * nt + i, hg)),
            pl.BlockSpec((1, hps, A_DK, A_DV), lambda b, hg, i: (b, hg, 0, 0)),
        ],
        out_shape=[
            jax.ShapeDtypeStruct((m, kd), out_dtype),
            jax.ShapeDtypeStruct((batch, A_HEADS, A_DK, A_DV), F32),
        ],
        scratch_shapes=[
            pltpu.VMEM((hps, A_DV, A_DK), F32),
            pltpu.VMEM((hps, c, A_DK), F32),
            pltpu.VMEM((hps, c, A_DK), F32),
            pltpu.VMEM((hps, c, A_DV), F32),
        ],
        compiler_params=_params("parallel", "parallel", "arbitrary"),
        name="hgrn2_gla",
    )(*args)


def _kv_kernel(x_ref, g_ref, wkt_ref, wvt_ref, wft_ref, bf_ref, kg_ref, *out_refs, with_rows):
    kt_ref, vt_ref, lft_ref, kb_ref, vbt_ref = out_refs[:5]
    h = _rms(x_ref[...], g_ref[...]).astype(BF16)
    tm = h.shape[0]
    kt = lax.dot_general(wkt_ref[...], h, _NT, preferred_element_type=F32)
    k3 = kt.reshape(B_HEADS, B_HEAD_DIM, tm)
    ms = jnp.mean(k3 * k3, axis=1, keepdims=True)
    k3 = k3 * lax.rsqrt(ms + RMS_EPS) * kg_ref[...][None]
    vt = lax.dot_general(wvt_ref[...], h, _NT, preferred_element_type=F32)
    fz = lax.dot_general(wft_ref[...], h, _NT, preferred_element_type=F32) + bf_ref[...]
    krow = k3.reshape(B_HEADS * B_HEAD_DIM, tm).T
    kt_ref[0] = k3
    vt_ref[0] = vt.reshape(B_HEADS, B_HEAD_DIM, tm)
    lft_ref[0] = jnp.minimum(fz, 0.0) - jnp.log1p(jnp.exp(-jnp.abs(fz)))
    kb_ref[...] = krow.astype(BF16)
    vbt_ref[0] = vt.reshape(B_HEADS, B_HEAD_DIM, tm).astype(BF16)
    if with_rows:
        krow_ref, vrow_ref = out_refs[5:]
        krow_ref[...] = krow
        vrow_ref[...] = vt.T


def _shared_kv(x, g, wkt, wvt, wft, bf, kg, *, batch, seq, tm, with_rows):
    m, d = x.shape
    hd = B_HEADS * B_HEAD_DIM
    nt = seq // tm
    t_spec = pl.BlockSpec((1, B_HEADS, B_HEAD_DIM, tm), lambda b, i: (b, 0, 0, i))
    row_spec = pl.BlockSpec((tm, hd), lambda b, i: (b * nt + i, 0))
    t_shape = (batch, B_HEADS, B_HEAD_DIM, seq)
    out_specs = [t_spec, t_spec, pl.BlockSpec((1, B_HEADS, tm), lambda b, i: (b, 0, i)), row_spec, t_spec]
    out_shape = [
        jax.ShapeDtypeStruct(t_shape, F32),
        jax.ShapeDtypeStruct(t_shape, F32),
        jax.ShapeDtypeStruct((batch, B_HEADS, seq), F32),
        jax.ShapeDtypeStruct((m, hd), BF16),
        jax.ShapeDtypeStruct(t_shape, BF16),
    ]
    if with_rows:
        out_specs += [row_spec, row_spec]
        out_shape += [jax.ShapeDtypeStruct((m, hd), F32)] * 2
    return pl.pallas_call(
        functools.partial(_kv_kernel, with_rows=with_rows),
        grid=(batch, nt),
        in_specs=[
            pl.BlockSpec((tm, d), lambda b, i: (b * nt + i, 0)),
            pl.BlockSpec((1, d), lambda b, i: (0, 0)),
            _resident((hd, d)),
            _resident((hd, d)),
            _resident((B_HEADS, d)),
            pl.BlockSpec((B_HEADS, 1), lambda b, i: (0, 0)),
            pl.BlockSpec((B_HEAD_DIM, 1), lambda b, i: (0, 0)),
        ],
        out_specs=out_specs,
        out_shape=out_shape,
        compiler_params=_params("parallel", "parallel"),
        name="shared_kv",
    )(x, g, wkt, wvt, wft, bf, kg)


def _qproj_kernel(x_ref, g_ref, wt_ref, qg_ref, o_ref):
    h = _rms(x_ref[...], g_ref[...]).astype(BF16)
    tm = h.shape[0]
    qt = lax.dot_general(wt_ref[...], h, _NT, preferred_element_type=F32)
    q3 = qt.reshape(B_HEADS, B_HEAD_DIM, tm)
    ms = jnp.mean(q3 * q3, axis=1, keepdims=True)
    q3 = q3 * lax.rsqrt(ms + RMS_EPS) * qg_ref[...][None] * (B_HEAD_DIM ** -0.5)
    o_ref[...] = q3.reshape(B_HEADS * B_HEAD_DIM, tm).astype(BF16)


def _qproj_t(x, g, wt3, layer, qg, *, tm):
    m, d = x.shape
    hd = wt3.shape[1]
    return pl.pallas_call(
        _qproj_kernel,
        grid=(m // tm,),
        in_specs=[
            pl.BlockSpec((tm, d), lambda i: (i, 0)),
            pl.BlockSpec((1, d), lambda i: (0, 0)),
            _resident_layer(wt3, layer),
            pl.BlockSpec((B_HEAD_DIM, 1), lambda i: (0, 0)),
        ],
        out_specs=pl.BlockSpec((hd, tm), lambda i: (0, i)),
        out_shape=jax.ShapeDtypeStruct((hd, m), BF16),
        compiler_params=_params("parallel"),
        name="qproj_t",
    )(x, g, wt3, qg)


def _cumsum_prompt_kernel(x_ref, row_ref, col_ref, carry_scr):
    @pl.when(pl.program_id(1) == 0)
    def _():
        carry_scr[...] = jnp.zeros_like(carry_scr)

    x = x_ref[0]
    t = x.shape[1]
    upper = lax.broadcasted_iota(jnp.int32, (t, t), 0) <= lax.broadcasted_iota(jnp.int32, (t, t), 1)
    row = _tri_matmul(x, upper, tri_on_left=False) + carry_scr[...]
    row_ref[0] = row
    col_ref[0] = row.T
    carry_scr[...] = row[:, t - 1:t]


def _cumsum_prompt(lft, *, t=256):
    b, h, l = lft.shape
    return pl.pallas_call(
        _cumsum_prompt_kernel,
        grid=(b, l // t),
        in_specs=[pl.BlockSpec((1, h, t), lambda i, j: (i, 0, j))],
        out_specs=[
            pl.BlockSpec((1, h, t), lambda i, j: (i, 0, j)),
            pl.BlockSpec((1, t, h), lambda i, j: (i, j, 0)),
        ],
        out_shape=[jax.ShapeDtypeStruct((b, h, l), F32), jax.ShapeDtypeStruct((b, l, h), F32)],
        scratch_shapes=[pltpu.VMEM((h, 1), F32)],
        compiler_params=_params("parallel", "arbitrary"),
        name="cumsum_prompt",
    )(lft)


def _fox_prefill_kernel(qt_ref, k_ref, vt_ref, ccol_ref, crow_ref, o_ref, m_scr, l_scr, acc_scr, *, tq, tk, hpg):
    grp = pl.program_id(1)
    qi = pl.program_id(2)
    row_head = lax.broadcasted_iota(jnp.int32, (2 * B_HEAD_DIM, 1), 0) // B_HEAD_DIM
    qm = []
    for pr in range(hpg // 2):
        qt = qt_ref[pr * 2 * B_HEAD_DIM:(pr + 1) * 2 * B_HEAD_DIM, :]
        qm += [jnp.where(row_head == hh, qt, jnp.zeros_like(qt)) for hh in range(2)]
    hl = lax.broadcasted_iota(jnp.int32, (1, B_HEADS), 1)
    nsub = tq // tk

    m_scr[...] = jnp.full_like(m_scr, -jnp.inf)
    l_scr[...] = jnp.zeros_like(l_scr)
    acc_scr[...] = jnp.zeros_like(acc_scr)

    def step(kb, diag):
        ks = pl.ds(pl.multiple_of(kb * tk, tk), tk)
        ccol = ccol_ref[0, ks, :]
        ys = []
        for hh in range(hpg):
            pr = hh // 2
            k = k_ref[ks, pr * 2 * B_HEAD_DIM:(pr + 1) * 2 * B_HEAD_DIM]
            ys.append(jnp.dot(k, qm[hh], preferred_element_type=F32))
        ps = []
        alphas = []
        for hh in range(hpg):
            ck = jnp.sum(jnp.where(hl == hpg * grp + hh, ccol, 0.0), axis=-1, keepdims=True)
            y = ys[hh] - ck
            if diag is not None:
                r = lax.broadcasted_iota(jnp.int32, (tk, tq), 0) + diag
                c = lax.broadcasted_iota(jnp.int32, (tk, tq), 1)
                y = jnp.where(r <= c, y, MASK_VALUE)
            cq = crow_ref[0, 0, hh:hh + 1, :]
            m_old = m_scr[hh]
            m_new = jnp.maximum(m_old, jnp.max(y, axis=0, keepdims=True) + cq)
            alpha = jnp.exp(m_old - m_new)
            p = jnp.exp(y - (m_new - cq))
            l_scr[hh] = alpha * l_scr[hh] + jnp.sum(p, axis=0, keepdims=True)
            m_scr[hh] = m_new
            ps.append(p.astype(BF16))
            alphas.append(alpha)
        for hh in range(hpg):
            vt = vt_ref[0, 0, hh * B_HEAD_DIM:(hh + 1) * B_HEAD_DIM, ks]
            acc_scr[hh] = alphas[hh] * acc_scr[hh] + jnp.dot(vt, ps[hh], preferred_element_type=F32)

    def body(kb, carry):
        step(kb, None)
        return carry

    lax.fori_loop(0, qi * nsub, body, 0)
    for j in range(nsub):
        step(qi * nsub + j, j * tk)
    ot = jnp.concatenate([acc_scr[hh] / l_scr[hh] for hh in range(hpg)], axis=0)
    o_ref[...] = ot.T.astype(o_ref.dtype)


def _fox_prefill(qt, kb, vbt, ccol, crow, *, batch, seq, tq, tk, hpg=4):
    m = batch * seq
    nq = seq // tq
    ngrp = B_HEADS // hpg
    gw = hpg * B_HEAD_DIM
    vbt4 = vbt.reshape(batch, ngrp, gw, seq)
    crow4 = crow.reshape(batch, ngrp, hpg, seq)
    return pl.pallas_call(
        functools.partial(_fox_prefill_kernel, tq=tq, tk=tk, hpg=hpg),
        grid=(batch, ngrp, nq),
        in_specs=[
            pl.BlockSpec((gw, tq), lambda b, p, i: (p, b * nq + i)),
            pl.BlockSpec((seq, gw), lambda b, p, i: (b, p)),
            pl.BlockSpec((1, 1, gw, seq), lambda b, p, i: (b, p, 0, 0)),
            pl.BlockSpec((1, seq, B_HEADS), lambda b, p, i: (b, 0, 0)),
            pl.BlockSpec((1, 1, hpg, tq), lambda b, p, i: (b, p, 0, i)),
        ],
        out_specs=pl.BlockSpec((tq, gw), lambda b, p, i: (b * nq + i, p)),
        out_shape=jax.ShapeDtypeStruct((m, B_HEADS * B_HEAD_DIM), BF16),
        scratch_shapes=[
            pltpu.VMEM((hpg, 1, tq), F32),
            pltpu.VMEM((hpg, 1, tq), F32),
            pltpu.VMEM((hpg, B_HEAD_DIM, tq), F32),
        ],
        compiler_params=_params("parallel", "parallel", "arbitrary"),
        name="fox_prefill",
    )(qt, kb, vbt4, ccol, crow4)


def _cumsum_sample_kernel(*refs, n_steps, ppb):
    page_refs = refs[1:1 + ppb]
    new_ref, past_ref, newrow_ref, carry_scr = refs[1 + ppb:]
    j = pl.program_id(1)
    h = carry_scr.shape[0]

    @pl.when(j == 0)
    def _():
        carry_scr[...] = jnp.zeros_like(carry_scr)

    def local_cumsum(x):
        t = x.shape[1]
        upper = (lax.broadcasted_iota(jnp.int32, (t, PAGE_SIZE), 0)
                 <= lax.broadcasted_iota(jnp.int32, (t, PAGE_SIZE), 1))
        return _tri_matmul(x, upper, tri_on_left=False)

    loc = local_cumsum(jnp.concatenate([r[0] for r in page_refs], axis=0))
    carry = carry_scr[...]
    for jj in range(ppb):
        page = loc[jj * h:(jj + 1) * h]
        past_ref[0, :, jj * PAGE_SIZE:(jj + 1) * PAGE_SIZE] = page + carry
        carry = carry + page[:, PAGE_SIZE - 1:PAGE_SIZE]
    carry_scr[...] = carry

    @pl.when(j == n_steps - 1)
    def _():
        newrow_ref[0] = local_cumsum(new_ref[0]) + carry


def _cumsum_sample(page_table_flat, cache_lft, lft_new, *, batch, n_pages, dec_seq, ppb=16):
    h = cache_lft.shape[1]
    n_steps = n_pages // ppb

    def page(jj):
        return lambda b, j, pt: (pt[b * n_pages + j * ppb + jj], 0, 0)

    grid_spec = pltpu.PrefetchScalarGridSpec(
        num_scalar_prefetch=1,
        grid=(batch, n_steps),
        in_specs=[pl.BlockSpec((1, h, PAGE_SIZE), page(jj)) for jj in range(ppb)]
        + [pl.BlockSpec((1, h, dec_seq), lambda b, j, pt: (b, 0, 0))],
        out_specs=[
            pl.BlockSpec((1, h, ppb * PAGE_SIZE), lambda b, j, pt: (b, 0, j)),
            pl.BlockSpec((1, h, PAGE_SIZE), lambda b, j, pt: (b, 0, 0)),
        ],
        scratch_shapes=[pltpu.VMEM((h, 1), F32)],
    )
    return pl.pallas_call(
        functools.partial(_cumsum_sample_kernel, n_steps=n_steps, ppb=ppb),
        grid_spec=grid_spec,
        out_shape=[
            jax.ShapeDtypeStruct((batch, h, n_pages * PAGE_SIZE), F32),
            jax.ShapeDtypeStruct((batch, h, PAGE_SIZE), F32),
        ],
        compiler_params=_params("parallel", "arbitrary"),
        name="cumsum_sample",
    )(page_table_flat, *([cache_lft] * ppb), lft_new)


def _fox_decode_kernel(*refs, ppb, dec_seq, n_steps):
    hq_ref, qg_ref = refs[1:3]
    k_refs = refs[3:3 + ppb]
    v_refs = refs[3 + ppb:3 + 2 * ppb]
    ck_ref, ckn_ref, cq_ref, kn_ref, vn_ref, o_ref = refs[3 + 2 * ppb:9 + 2 * ppb]
    qbd_scr, m_scr, l_scr, acc_scr, bias_scr, knp_scr, vnp_scr = refs[9 + 2 * ppb:]
    s_id = pl.program_id(1)
    hd = B_HEADS * B_HEAD_DIM
    nrow = B_HEADS * dec_seq
    lane_head = lax.broadcasted_iota(jnp.int32, (1, hd), 1) // B_HEAD_DIM

    @pl.when(s_id == 0)
    def _():
        qn = _head_rms(hq_ref[...], qg_ref[...]) * (B_HEAD_DIM ** -0.5)
        for h2 in range(B_HEADS // 2):
            a = jnp.where(lane_head == 2 * h2, qn, 0.0)
            b = jnp.where(lane_head == 2 * h2 + 1, qn, 0.0)
            qbd_scr[2 * h2 * dec_seq:(2 * h2 + 2) * dec_seq, :] = jnp.concatenate([a, b], axis=0).astype(BF16)
        m_scr[...] = jnp.full_like(m_scr, -jnp.inf)
        l_scr[...] = jnp.zeros_like(l_scr)
        acc_scr[...] = jnp.zeros_like(acc_scr)
        knp_scr[...] = jnp.zeros_like(knp_scr)
        vnp_scr[...] = jnp.zeros_like(vnp_scr)
        knp_scr[:, 0:dec_seq] = kn_ref[0]
        vnp_scr[:, 0:dec_seq] = vn_ref[0]

    cq = cq_ref[0]

    def attend(kt_tiles, vt_tiles, ck_rows, mask):
        n = len(kt_tiles)
        qbd = qbd_scr[...]
        groups = [list(range(g, min(g + 2, n))) for g in range(0, n, 2)]
        s = jnp.concatenate(
            [jnp.dot(qbd, jnp.concatenate([kt_tiles[t] for t in g], axis=1), preferred_element_type=F32) for g in groups],
            axis=1)
        for h in range(B_HEADS):
            rs = slice(h * dec_seq, (h + 1) * dec_seq)
            bias_scr[rs, 0:n * PAGE_SIZE] = cq[rs] - ck_rows[h:h + 1, :]
        s = s + bias_scr[:, 0:n * PAGE_SIZE]
        if mask is not None:
            s = jnp.where(mask, s, MASK_VALUE)
        m_old = m_scr[...]
        m_new = jnp.maximum(m_old, jnp.max(s, axis=-1, keepdims=True))
        alpha = jnp.exp(m_old - m_new)
        p = jnp.exp(s - m_new)
        l_scr[...] = alpha * l_scr[...] + jnp.sum(p, axis=-1, keepdims=True)
        pv = None
        for g in groups:
            t = lax.dot_general(p[:, g[0] * PAGE_SIZE:(g[-1] + 1) * PAGE_SIZE].astype(BF16),
                                jnp.concatenate([vt_tiles[t] for t in g], axis=1), _NT, preferred_element_type=F32)
            pv = t if pv is None else pv + t
        acc_scr[...] = alpha * acc_scr[...] + pv
        m_scr[...] = m_new

    attend([r[0].reshape(hd, PAGE_SIZE).astype(BF16) for r in k_refs],
           [r[0].reshape(hd, PAGE_SIZE).astype(BF16) for r in v_refs], ck_ref[0], None)

    @pl.when(s_id == n_steps - 1)
    def _():
        r = lax.broadcasted_iota(jnp.int32, (nrow, PAGE_SIZE), 0) % dec_seq
        c = lax.broadcasted_iota(jnp.int32, (nrow, PAGE_SIZE), 1)
        attend([knp_scr[...].astype(BF16)], [vnp_scr[...].astype(BF16)], ckn_ref[0], c <= r)
        o = acc_scr[...] / l_scr[...]
        out = jnp.zeros((dec_seq, hd), F32)
        for h in range(B_HEADS):
            out = out + jnp.where(lane_head == h, o[h * dec_seq:(h + 1) * dec_seq, :], 0.0)
        o_ref[...] = out.astype(o_ref.dtype)


def _fox_decode(page_table_flat, hq, qg_full, cache_kt, cache_vt, ck_past, ck_new, cq_col, kt_new, vt_new,
                *, batch, n_pages, dec_seq, ppb):
    hd = B_HEADS * B_HEAD_DIM
    n_steps = n_pages // ppb
    nrow = B_HEADS * dec_seq

    def page(jj):
        return lambda b, s, pt: (pt[b * n_pages + s * ppb + jj], 0, 0, 0)

    page_block = (1, B_HEADS, B_HEAD_DIM, PAGE_SIZE)
    in_specs = [
        pl.BlockSpec((dec_seq, hd), lambda b, s, pt: (b, 0)),
        pl.BlockSpec((1, hd), lambda b, s, pt: (0, 0)),
    ]
    in_specs += [pl.BlockSpec(page_block, page(jj)) for jj in range(ppb)]
    in_specs += [pl.BlockSpec(page_block, page(jj)) for jj in range(ppb)]
    in_specs += [
        pl.BlockSpec((1, B_HEADS, ppb * PAGE_SIZE), lambda b, s, pt: (b, 0, s)),
        pl.BlockSpec((1, B_HEADS, PAGE_SIZE), lambda b, s, pt: (b, 0, 0)),
        pl.BlockSpec((1, nrow, 1), lambda b, s, pt: (b, 0, 0)),
        pl.BlockSpec((1, hd, dec_seq), lambda b, s, pt: (b, 0, 0)),
        pl.BlockSpec((1, hd, dec_seq), lambda b, s, pt: (b, 0, 0)),
    ]
    grid_spec = pltpu.PrefetchScalarGridSpec(
        num_scalar_prefetch=1,
        grid=(batch, n_steps),
        in_specs=in_specs,
        out_specs=pl.BlockSpec((dec_seq, hd), lambda b, s, pt: (b, 0)),
        scratch_shapes=[
            pltpu.VMEM((nrow, hd), BF16),
            pltpu.VMEM((nrow, 1), F32),
            pltpu.VMEM((nrow, 1), F32),
            pltpu.VMEM((nrow, hd), F32),
            pltpu.VMEM((nrow, ppb * PAGE_SIZE), F32),
            pltpu.VMEM((hd, PAGE_SIZE), F32),
            pltpu.VMEM((hd, PAGE_SIZE), F32),
        ],
    )
    args = ([page_table_flat, hq, qg_full] + [cache_kt] * ppb + [cache_vt] * ppb
            + [ck_past, ck_new, cq_col, kt_new, vt_new])
    return pl.pallas_call(
        functools.partial(_fox_decode_kernel, ppb=ppb, dec_seq=dec_seq, n_steps=n_steps),
        grid_spec=grid_spec,
        out_shape=jax.ShapeDtypeStruct((batch * dec_seq, hd), F32),
        compiler_params=_params("parallel", "arbitrary"),
        name="fox_decode",
    )(*args)


def _trunk(x, batch, seq, states, w, attention, *, tm, kv_tm, kv_rows, gla_cfg):
    n_a = w["a_w_in"].shape[0]
    depth = w["wgu"].shape[0]
    new_states = []
    kv = ctx = None
    for layer in range(depth):
        if layer < n_a:
            h = _norm_matmul(x, w["a_norm"][layer:layer + 1], w["a_w_in"], layer, tm=tm, tn=1024)
            s0 = None if states is None else states[layer]
            og, s_new = _gla(h, w["a_lb"], w["a_onorm"][layer:layer + 1], s0, layer=layer, batch=batch, seq=seq, **gla_cfg)
            new_states.append(s_new)
            x = _matmul_res(og, w["a_w_out"], layer, x, tm=tm)
        else:
            j = layer - n_a
            if j == 0:
                kv = _shared_kv(x, w["kv_norm"], w["wkt"], w["wvt"], w["wft"], w["bf"], w["kg"],
                                batch=kv_rows[0], seq=kv_rows[1], tm=kv_tm, with_rows=attention.with_rows)
                ctx = attention.prepare(kv)
            o = attention.run(j, x, w, ctx, kv)
            x = _matmul_res(o, w["b_wo"], j, x, tm=tm)
        x = _ffn(x, w["ffn_norm"][layer:layer + 1], w["wgu"], w["wd"], layer, tm=min(tm, 512))
    return x, jnp.stack(new_states), kv


class _PromptAttention:
    with_rows = False

    def __init__(self, batch, seq, tq, tk):
        self.batch, self.seq, self.tq, self.tk = batch, seq, tq, tk

    def prepare(self, kv):
        return _cumsum_prompt(kv[2])

    def run(self, j, x, w, ctx, kv):
        crow, ccol = ctx
        qt = _qproj_t(x, w["b_norm"][j:j + 1], w["b_wq_t"], j, w["qg_col"][j], tm=512)
        return _fox_prefill(qt, kv[3], kv[4], ccol, crow, batch=self.batch, seq=self.seq, tq=self.tq, tk=self.tk)


class _SampleAttention:
    with_rows = True

    def __init__(self, batch, dec_seq, n_pages, pt_flat, cache_kt, cache_vt, cache_lft, ppb):
        self.batch, self.dec_seq, self.n_pages, self.pt = batch, dec_seq, n_pages, pt_flat
        self.cache_kt, self.cache_vt, self.cache_lft, self.ppb = cache_kt, cache_vt, cache_lft, ppb

    def _per_batch(self, t):
        rows = t.shape[0]
        return t.reshape(rows, self.batch, self.dec_seq).transpose(1, 0, 2)

    def prepare(self, kv):
        hd = B_HEADS * B_HEAD_DIM
        lft_new = self._per_batch(kv[2][0])
        ck_past, ck_new = _cumsum_sample(self.pt, self.cache_lft, lft_new, batch=self.batch, n_pages=self.n_pages,
                                         dec_seq=self.dec_seq)
        cq_col = ck_new[:, :, :self.dec_seq].reshape(self.batch, B_HEADS * self.dec_seq, 1)
        kt_new = self._per_batch(kv[0].reshape(hd, -1))
        vt_new = self._per_batch(kv[1].reshape(hd, -1))
        return ck_past, ck_new, cq_col, kt_new, vt_new

    def run(self, j, x, w, ctx, kv):
        ck_past, ck_new, cq_col, kt_new, vt_new = ctx
        hq = _norm_matmul(x, w["b_norm"][j:j + 1], w["b_wq"], j, tm=x.shape[0], tn=1024)
        return _fox_decode(self.pt, hq, w["qg_full"][j:j + 1], self.cache_kt, self.cache_vt, ck_past, ck_new, cq_col,
                           kt_new, vt_new, batch=self.batch, n_pages=self.n_pages, dec_seq=self.dec_seq, ppb=self.ppb)


def kernel(x_prompt, x_sample, state_hgrn, cache_k, cache_v, cache_logf, page_table, a_norm, a_w_in, a_lb, a_onorm, a_w_out, kv_norm, w_kvf, b_f, k_norm, b_norm, b_wq, q_norm, b_wo, ffn_norm, w_gate_up, w_down):
    batch, seq, d = x_prompt.shape
    dec_batch, dec_seq, _ = x_sample.shape
    n_pages = page_table.shape[1]
    hd = B_HEADS * B_HEAD_DIM
    assert seq % A_CHUNK == 0 and dec_seq <= A_CHUNK

    w_kvf_t = w_kvf.T.astype(BF16)
    w = dict(
        a_norm=a_norm, a_w_in=a_w_in.astype(BF16), a_lb=a_lb.astype(F32), a_onorm=a_onorm,
        a_w_out=a_w_out.astype(BF16), kv_norm=kv_norm.reshape(1, d),
        wkt=w_kvf_t[:hd], wvt=w_kvf_t[hd:2 * hd], wft=w_kvf_t[2 * hd:],
        bf=b_f.astype(F32).reshape(B_HEADS, 1), kg=k_norm.astype(F32).reshape(B_HEAD_DIM, 1),
        b_norm=b_norm, b_wq=b_wq.astype(BF16), b_wo=b_wo.astype(BF16), ffn_norm=ffn_norm,
        wgu=w_gate_up.astype(BF16), wd=w_down.astype(BF16),
    )
    w["b_wq_t"] = jnp.swapaxes(b_wq, 1, 2).astype(BF16)
    w["qg_col"] = q_norm.astype(F32)[:, :, None]
    w["qg_full"] = jnp.tile(q_norm, (1, B_HEADS))

    prompt_att = _PromptAttention(batch, seq, tq=512, tk=512)
    y_p, hgrn_p, kv_p = _trunk(
        x_prompt.reshape(batch * seq, d), batch, seq, None, w, prompt_att, tm=1024, kv_tm=512, kv_rows=(batch, seq),
        gla_cfg=dict(c=A_CHUNK, u=A_SUB, t=256, hps=8, out_dtype=BF16))
    k_p = kv_p[0].transpose(0, 3, 1, 2)
    v_p = kv_p[1].transpose(0, 3, 1, 2)
    lf_p = kv_p[2].transpose(0, 2, 1)

    m_s = dec_batch * dec_seq
    pt_flat = page_table.reshape(-1).astype(jnp.int32)
    sample_att = _SampleAttention(dec_batch, dec_seq, n_pages, pt_flat,
                                  cache_k.transpose(0, 2, 3, 1), cache_v.transpose(0, 2, 3, 1),
                                  cache_logf.transpose(0, 2, 1), ppb=8)
    y_s, hgrn_s, kv_s = _trunk(
        x_sample.reshape(m_s, d), dec_batch, dec_seq, state_hgrn, w, sample_att, tm=m_s, kv_tm=m_s, kv_rows=(1, m_s),
        gla_cfg=dict(c=dec_seq, u=min(A_SUB, dec_seq), t=dec_seq, hps=A_HEADS, out_dtype=F32))
    k_s = kv_s[5].reshape(dec_batch, dec_seq, B_HEADS, B_HEAD_DIM)
    v_s = kv_s[6].reshape(dec_batch, dec_seq, B_HEADS, B_HEAD_DIM)
    lf_s = kv_s[2][0].T.reshape(dec_batch, dec_seq, B_HEADS)

    return (y_p.reshape(batch, seq, d), y_s.reshape(dec_batch, dec_seq, d), hgrn_p, k_p, v_p, lf_p,
            hgrn_s, k_s, v_s, lf_s)
```

```python
import functools

import jax
import jax.numpy as jnp
from jax import lax
from jax.experimental import pallas as pl
from jax.experimental.pallas import tpu as pltpu

F32 = jnp.float32
BF16 = jnp.bfloat16
HIGHEST = lax.Precision.HIGHEST

D_MODEL = 1024
A_HEADS = 8
A_DK = 128
A_DV = 128
A_CHUNK = 64
A_SUB = 16
B_HEADS = 16
B_HEAD_DIM = 64
PAGE_SIZE = 128
RMS_EPS = 1e-6
MASK_VALUE = -1e30
MIN_F = 1e-30
LOG2E = 1.4426950408889634
LANES = 128
SUBLANES = 8
VMEM_LIMIT = 56 * 1024 * 1024

_NT = (((1,), (1,)), ((), ()))
_TN = (((0,), (0,)), ((), ()))


def _params(*sem):
    return pltpu.CompilerParams(dimension_semantics=sem, vmem_limit_bytes=VMEM_LIMIT)


def _rms(x, g):
    return x * lax.rsqrt(jnp.mean(x * x, axis=-1, keepdims=True) + RMS_EPS) * g


def _tri_matmul(x, tri, *, tri_on_left):
    hi = x.astype(BF16)
    r1 = x - hi.astype(F32)
    mid = r1.astype(BF16)
    lo = (r1 - mid.astype(F32)).astype(BF16)
    t = tri.astype(BF16)
    out = None
    for piece in (hi, mid, lo):
        d = jnp.dot(t, piece, preferred_element_type=F32) if tri_on_left else jnp.dot(piece, t, preferred_element_type=F32)
        out = d if out is None else out + d
    return out


def _resident(shape):
    return pl.BlockSpec(shape, lambda *_: (0,) * len(shape), pipeline_mode=pl.Buffered(1))


def _resident_layer(w3, layer, col_block=0, cols=None):
    _, r, c = w3.shape
    return pl.BlockSpec((None, r, cols or c), lambda *_: (layer, 0, col_block), pipeline_mode=pl.Buffered(1))


def _group_indicator(n, groups_padded, group):
    r = lax.broadcasted_iota(jnp.int32, (n, groups_padded), 0)
    c = lax.broadcasted_iota(jnp.int32, (n, groups_padded), 1)
    return (r // group == c).astype(F32)


def _head_rms(x, gain_tiled):
    n = x.shape[-1]
    g = _group_indicator(n, LANES, B_HEAD_DIM)
    ss = jnp.dot(x * x, g, preferred_element_type=F32, precision=HIGHEST)
    rs = lax.rsqrt(ss * (1.0 / B_HEAD_DIM) + RMS_EPS)
    rs_full = lax.dot_general(rs, g, _NT, preferred_element_type=F32, precision=HIGHEST)
    return x * rs_full * gain_tiled


def _norm_matmul_kernel(x_ref, g_ref, w_ref, o_ref, xn_ref):
    @pl.when(pl.program_id(1) == 0)
    def _():
        xn_ref[...] = _rms(x_ref[...], g_ref[...]).astype(BF16)

    o_ref[...] = jnp.dot(xn_ref[...], w_ref[...], preferred_element_type=F32).astype(o_ref.dtype)


def _norm_matmul(x, g, w3, layer, *, tm, tn, out_dtype=F32):
    m, k = x.shape
    n = w3.shape[2]
    return pl.pallas_call(
        _norm_matmul_kernel,
        grid=(m // tm, n // tn),
        in_specs=[
            pl.BlockSpec((tm, k), lambda i, j: (i, 0)),
            pl.BlockSpec((1, k), lambda i, j: (0, 0)),
            pl.BlockSpec((None, k, tn), lambda i, j: (layer, 0, j)),
        ],
        out_specs=pl.BlockSpec((tm, tn), lambda i, j: (i, j)),
        out_shape=jax.ShapeDtypeStruct((m, n), out_dtype),
        scratch_shapes=[pltpu.VMEM((tm, k), BF16)],
        compiler_params=_params("parallel", "arbitrary"),
        name="norm_matmul",
    )(x, g, w3)


def _ffn_kernel(a_ref, wo_ref, r_ref, g_ref, wg_ref, wu_ref, wd_ref, o_ref, h_ref, act_ref, *, tf):
    x = r_ref[...] + jnp.dot(a_ref[...].astype(BF16), wo_ref[...], preferred_element_type=F32)
    h_ref[...] = _rms(x, g_ref[...]).astype(BF16)
    for j in range(act_ref.shape[1] // tf):
        sl = slice(j * tf, (j + 1) * tf)
        a = jnp.dot(h_ref[...], wg_ref[:, sl], preferred_element_type=F32)
        b = jnp.dot(h_ref[...], wu_ref[:, sl], preferred_element_type=F32)
        act_ref[:, sl] = (jax.nn.silu(a) * b).astype(BF16)
    o_ref[...] = x + jnp.dot(act_ref[...], wd_ref[...], preferred_element_type=F32)


def _ffn(a, wo3, wo_layer, res, g, wgu3, wd3, layer, *, tm, tf=256):
    m, d = res.shape
    dff = wd3.shape[1]
    return pl.pallas_call(
        functools.partial(_ffn_kernel, tf=tf),
        grid=(m // tm,),
        in_specs=[
            pl.BlockSpec((tm, a.shape[1]), lambda i: (i, 0)),
            _resident_layer(wo3, wo_layer),
            pl.BlockSpec((tm, d), lambda i: (i, 0)),
            pl.BlockSpec((1, d), lambda i: (0, 0)),
            _resident_layer(wgu3, layer, 0, dff),
            _resident_layer(wgu3, layer, 1, dff),
            _resident_layer(wd3, layer),
        ],
        out_specs=pl.BlockSpec((tm, d), lambda i: (i, 0)),
        out_shape=jax.ShapeDtypeStruct((m, d), F32),
        scratch_shapes=[pltpu.VMEM((tm, d), BF16), pltpu.VMEM((tm, dff), BF16)],
        compiler_params=_params("parallel"),
        name="ffn",
    )(a, wo3, res, g, wgu3, wgu3, wd3)


def _gla_gates(z, lb, c):
    sig = jax.nn.sigmoid(z)
    f = lb + (1.0 - lb) * sig
    logf = jnp.log(jnp.maximum(f, MIN_F))
    k = (1.0 - lb) * jax.nn.sigmoid(-z)
    tri = lax.broadcasted_iota(jnp.int32, (c, c), 0) >= lax.broadcasted_iota(jnp.int32, (c, c), 1)
    return k, _tri_matmul(logf, tri, tri_on_left=True)


def _gla_diag(q, b2, b_scr, k_scr, v_scr, *, c, u):
    rows = lax.broadcasted_iota(jnp.int32, (SUBLANES, 1), 0)
    parts = []
    for i in range(c // u):
        lo = u * i
        for j in range(u // SUBLANES):
            p0 = lo + j * SUBLANES
            qp = q[p0:p0 + SUBLANES]
            bp = b2[p0:p0 + SUBLANES]
            acc = jnp.zeros((SUBLANES, A_DV), F32)
            for s in range(lo, p0 + SUBLANES):
                d = bp - b_scr[s:s + 1, :]
                if s >= p0:
                    d = jnp.minimum(d, 0.0)
                a = jnp.sum(qp * k_scr[s:s + 1, :] * jnp.exp2(d), axis=-1, keepdims=True)
                if s >= p0:
                    a = jnp.where(rows >= s - p0, a, 0.0)
                acc = acc + a * v_scr[s:s + 1, :]
            parts.append(acc)
    return parts


def _gla_mxu(q, k, b, v, st, parts, *, c, u):
    vb = v.astype(BF16)
    o_state = lax.dot_general((q * jnp.exp(b)).astype(BF16), st.astype(BF16), _NT, preferred_element_type=F32)
    parts = list(parts)
    for i in range(1, c // u):
        lo = u * i
        r = b[lo - 1:lo, :]
        qt = (q[lo:lo + u] * jnp.exp(b[lo:lo + u] - r)).astype(BF16)
        kt = (k[:lo] * jnp.exp(r - b[:lo])).astype(BF16)
        att = lax.dot_general(qt, kt, _NT, preferred_element_type=F32)
        off = jnp.dot(att.astype(BF16), vb[:lo], preferred_element_type=F32)
        for j in range(u // SUBLANES):
            idx = (lo + j * SUBLANES) // SUBLANES
            parts[idx] = parts[idx] + off[j * SUBLANES:(j + 1) * SUBLANES]
    o = (parts[0] if len(parts) == 1 else jnp.concatenate(parts, axis=0)) + o_state
    bl = b[c - 1:c, :]
    kd = (k * jnp.exp(bl - b)).astype(BF16)
    st_new = st * jnp.exp(bl) + lax.dot_general(vb, kd, _TN, preferred_element_type=F32)
    return o, st_new


def _gla_kernel(*refs, layer, c, u, n_chunks, hps, has_s0):
    if has_s0:
        q_ref, z_ref, v_ref, gt_ref, lb_ref, og_ref, s0_ref, o_ref, so_ref, st_ref, b_scr, k_scr, v_scr = refs
    else:
        q_ref, z_ref, v_ref, gt_ref, lb_ref, og_ref, o_ref, so_ref, st_ref, b_scr, k_scr, v_scr = refs
        s0_ref = None
    i = pl.program_id(2)

    @pl.when(i == 0)
    def _():
        for hh in range(hps):
            st_ref[hh] = s0_ref[0, hh].T if has_s0 else jnp.zeros((A_DV, A_DK), F32)

    a = lb_ref[...]
    e = jnp.exp(a - jnp.max(a, axis=0, keepdims=True))
    p = e / jnp.sum(e, axis=0, keepdims=True)
    lb_all = jnp.sum(p[:layer + 1], axis=0, keepdims=True) - p[0:1]
    og = og_ref[...]

    def body(ci, carry):
        rs = pl.ds(pl.multiple_of(ci * c, c), c)
        cols = [slice(hh * LANES, (hh + 1) * LANES) for hh in range(hps)]
        kb = [_gla_gates(z_ref[rs, cs], lb_all[:, cs], c) for cs in cols]
        for hh, cs in enumerate(cols):
            b_scr[hh] = kb[hh][1] * LOG2E
            k_scr[hh] = kb[hh][0]
            v_scr[hh] = v_ref[rs, cs]
        parts = [_gla_diag(q_ref[rs, cs], b_scr[hh], b_scr.at[hh], k_scr.at[hh], v_scr.at[hh], c=c, u=u)
                 for hh, cs in enumerate(cols)]
        for hh, cs in enumerate(cols):
            o, st_new = _gla_mxu(q_ref[rs, cs], kb[hh][0], kb[hh][1], v_ref[rs, cs], st_ref[hh], parts[hh], c=c, u=u)
            st_ref[hh] = st_new
            on = o * lax.rsqrt(jnp.mean(o * o, axis=-1, keepdims=True) + RMS_EPS) * og
            o_ref[rs, cs] = (on * jax.nn.silu(gt_ref[rs, cs])).astype(o_ref.dtype)
        return carry

    if n_chunks == 1:
        body(0, 0)
    else:
        lax.fori_loop(0, n_chunks, body, 0)

    @pl.when(i == pl.num_programs(2) - 1)
    def _():
        for hh in range(hps):
            so_ref[0, hh] = st_ref[hh].T


def _gla(h, a_lb, o_gain, s0, *, layer, batch, seq, c, u, t, hps, out_dtype):
    m = batch * seq
    nt = seq // t
    hb = A_HEADS // hps
    w = hps * LANES
    kd = A_HEADS * A_DK
    nblk = kd // w

    def col(off):
        return lambda b, hg, i: (b * nt + i, off * nblk + hg)

    in_specs = [
        pl.BlockSpec((t, w), col(0)),
        pl.BlockSpec((t, w), col(1)),
        pl.BlockSpec((t, w), col(2)),
        pl.BlockSpec((t, w), col(3)),
        pl.BlockSpec((a_lb.shape[0], w), lambda b, hg, i: (0, hg)),
        pl.BlockSpec((1, A_DV), lambda b, hg, i: (0, 0)),
    ]
    args = [h, h, h, h, a_lb, o_gain]
    if s0 is not None:
        in_specs.append(pl.BlockSpec((1, hps, A_DK, A_DV), lambda b, hg, i: (b, hg, 0, 0)))
        args.append(s0)
    kern = functools.partial(_gla_kernel, layer=layer, c=c, u=u, n_chunks=t // c, hps=hps, has_s0=s0 is not None)
    return pl.pallas_call(
        kern,
        grid=(batch, hb, nt),
        in_specs=in_specs,
        out_specs=[
            pl.BlockSpec((t, w), lambda b, hg, i: (b * nt + i, hg)),
            pl.BlockSpec((1, hps, A_DK, A_DV), lambda b, hg, i: (b, hg, 0, 0)),
        ],
        out_shape=[
            jax.ShapeDtypeStruct((m, kd), out_dtype),
            jax.ShapeDtypeStruct((batch, A_HEADS, A_DK, A_DV), F32),
        ],
        scratch_shapes=[
            pltpu.VMEM((hps, A_DV, A_DK), F32),
            pltpu.VMEM((hps, c, A_DK), F32),
            pltpu.VMEM((hps, c, A_DK), F32),
            pltpu.VMEM((hps, c, A_DV), F32),
        ],
        compiler_params=_params("parallel", "parallel", "arbitrary"),
        name="hgrn2_gla",
    )(*args)


def _kv_kernel(x_ref, g_ref, wkt_ref, wvt_ref, wft_ref, bf_ref, kg_ref, *out_refs, with_rows):
    kt_ref, vt_ref, lft_ref, kb_ref, vbt_ref = out_refs[:5]
    h = _rms(x_ref[...], g_ref[...]).astype(BF16)
    tm = h.shape[0]
    kt = lax.dot_general(wkt_ref[...], h, _NT, preferred_element_type=F32)
    k3 = kt.reshape(B_HEADS, B_HEAD_DIM, tm)
    ms = jnp.mean(k3 * k3, axis=1, keepdims=True)
    k3 = k3 * lax.rsqrt(ms + RMS_EPS) * kg_ref[...][None]
    vt = lax.dot_general(wvt_ref[...], h, _NT, preferred_element_type=F32)
    fz = lax.dot_general(wft_ref[...], h, _NT, preferred_element_type=F32) + bf_ref[...]
    krow = k3.reshape(B_HEADS * B_HEAD_DIM, tm).T
    kt_ref[0] = k3
    vt_ref[0] = vt.reshape(B_HEADS, B_HEAD_DIM, tm)
    lft_ref[0] = jnp.minimum(fz, 0.0) - jnp.log1p(jnp.exp(-jnp.abs(fz)))
    kb_ref[...] = krow.astype(BF16)
    vbt_ref[0] = vt.reshape(B_HEADS, B_HEAD_DIM, tm).astype(BF16)
    if with_rows:
        krow_ref, vrow_ref = out_refs[5:]
        krow_ref[...] = krow
        vrow_ref[...] = vt.T


def _shared_kv(x, g, wkt, wvt, wft, bf, kg, *, batch, seq, tm, with_rows):
    m, d = x.shape
    hd = B_HEADS * B_HEAD_DIM
    nt = seq // tm
    t_spec = pl.BlockSpec((1, B_HEADS, B_HEAD_DIM, tm), lambda b, i: (b, 0, 0, i))
    row_spec = pl.BlockSpec((tm, hd), lambda b, i: (b * nt + i, 0))
    t_shape = (batch, B_HEADS, B_HEAD_DIM, seq)
    out_specs = [t_spec, t_spec, pl.BlockSpec((1, B_HEADS, tm), lambda b, i: (b, 0, i)), row_spec, t_spec]
    out_shape = [
        jax.ShapeDtypeStruct(t_shape, F32),
        jax.ShapeDtypeStruct(t_shape, F32),
        jax.ShapeDtypeStruct((batch, B_HEADS, seq), F32),
        jax.ShapeDtypeStruct((m, hd), BF16),
        jax.ShapeDtypeStruct(t_shape, BF16),
    ]
    if with_rows:
        out_specs += [row_spec, row_spec]
        out_shape += [jax.ShapeDtypeStruct((m, hd), F32)] * 2
    return pl.pallas_call(
        functools.partial(_kv_kernel, with_rows=with_rows),
        grid=(batch, nt),
        in_specs=[
            pl.BlockSpec((tm, d), lambda b, i: (b * nt + i, 0)),
            pl.BlockSpec((1, d), lambda b, i: (0, 0)),
            _resident((hd, d)),
            _resident((hd, d)),
            _resident((B_HEADS, d)),
            pl.BlockSpec((B_HEADS, 1), lambda b, i: (0, 0)),
            pl.BlockSpec((B_HEAD_DIM, 1), lambda b, i: (0, 0)),
        ],
        out_specs=out_specs,
        out_shape=out_shape,
        compiler_params=_params("parallel", "parallel"),
        name="shared_kv",
    )(x, g, wkt, wvt, wft, bf, kg)


def _qproj_kernel(x_ref, g_ref, wt_ref, qg_ref, o_ref):
    h = _rms(x_ref[...], g_ref[...]).astype(BF16)
    tm = h.shape[0]
    qt = lax.dot_general(wt_ref[...], h, _NT, preferred_element_type=F32)
    q3 = qt.reshape(B_HEADS, B_HEAD_DIM, tm)
    ms = jnp.mean(q3 * q3, axis=1, keepdims=True)
    q3 = q3 * lax.rsqrt(ms + RMS_EPS) * qg_ref[...][None] * (B_HEAD_DIM ** -0.5)
    o_ref[...] = q3.reshape(B_HEADS * B_HEAD_DIM, tm).astype(BF16)


def _qproj_t(x, g, wt3, layer, qg, *, tm):
    m, d = x.shape
    hd = wt3.shape[1]
    return pl.pallas_call(
        _qproj_kernel,
        grid=(m // tm,),
        in_specs=[
            pl.BlockSpec((tm, d), lambda i: (i, 0)),
            pl.BlockSpec((1, d), lambda i: (0, 0)),
            _resident_layer(wt3, layer),
            pl.BlockSpec((B_HEAD_DIM, 1), lambda i: (0, 0)),
        ],
        out_specs=pl.BlockSpec((hd, tm), lambda i: (0, i)),
        out_shape=jax.ShapeDtypeStruct((hd, m), BF16),
        compiler_params=_params("parallel"),
        name="qproj_t",
    )(x, g, wt3, qg)


def _cumsum_prompt_kernel(x_ref, row_ref, col_ref, carry_scr):
    @pl.when(pl.program_id(1) == 0)
    def _():
        carry_scr[...] = jnp.zeros_like(carry_scr)

    x = x_ref[0]
    t = x.shape[1]
    upper = lax.broadcasted_iota(jnp.int32, (t, t), 0) <= lax.broadcasted_iota(jnp.int32, (t, t), 1)
    row = _tri_matmul(x, upper, tri_on_left=False) + carry_scr[...]
    row_ref[0] = row
    col_ref[0] = row.T
    carry_scr[...] = row[:, t - 1:t]


def _cumsum_prompt(lft, *, t=256):
    b, h, l = lft.shape
    return pl.pallas_call(
        _cumsum_prompt_kernel,
        grid=(b, l // t),
        in_specs=[pl.BlockSpec((1, h, t), lambda i, j: (i, 0, j))],
        out_specs=[
            pl.BlockSpec((1, h, t), lambda i, j: (i, 0, j)),
            pl.BlockSpec((1, t, h), lambda i, j: (i, j, 0)),
        ],
        out_shape=[jax.ShapeDtypeStruct((b, h, l), F32), jax.ShapeDtypeStruct((b, l, h), F32)],
        scratch_shapes=[pltpu.VMEM((h, 1), F32)],
        compiler_params=_params("parallel", "arbitrary"),
        name="cumsum_prompt",
    )(lft)


def _fox_prefill_kernel(qt_ref, k_ref, vt_ref, ccol_ref, crow_ref, o_ref, m_scr, l_scr, acc_scr, *, tq, tk, hpg):
    grp = pl.program_id(1)
    qi = pl.program_id(2)
    row_head = lax.broadcasted_iota(jnp.int32, (2 * B_HEAD_DIM, 1), 0) // B_HEAD_DIM
    qm = []
    for pr in range(hpg // 2):
        qt = qt_ref[pr * 2 * B_HEAD_DIM:(pr + 1) * 2 * B_HEAD_DIM, :]
        qm += [jnp.where(row_head == hh, qt, jnp.zeros_like(qt)) for hh in range(2)]
    hl = lax.broadcasted_iota(jnp.int32, (1, B_HEADS), 1)
    nsub = tq // tk

    m_scr[...] = jnp.full_like(m_scr, -jnp.inf)
    l_scr[...] = jnp.zeros_like(l_scr)
    acc_scr[...] = jnp.zeros_like(acc_scr)

    def step(kb, diag):
        ks = pl.ds(pl.multiple_of(kb * tk, tk), tk)
        ccol = ccol_ref[0, ks, :]
        ys = []
        for hh in range(hpg):
            pr = hh // 2
            k = k_ref[ks, pr * 2 * B_HEAD_DIM:(pr + 1) * 2 * B_HEAD_DIM]
            ys.append(jnp.dot(k, qm[hh], preferred_element_type=F32))
        ps = []
        alphas = []
        for hh in range(hpg):
            ck = jnp.sum(jnp.where(hl == hpg * grp + hh, ccol, 0.0), axis=-1, keepdims=True)
            y = ys[hh] - ck
            if diag is not None:
                r = lax.broadcasted_iota(jnp.int32, (tk, tq), 0) + diag
                c = lax.broadcasted_iota(jnp.int32, (tk, tq), 1)
                y = jnp.where(r <= c, y, MASK_VALUE)
            cq = crow_ref[0, 0, hh:hh + 1, :]
            m_old = m_scr[hh]
            m_new = jnp.maximum(m_old, jnp.max(y, axis=0, keepdims=True) + cq)
            alpha = jnp.exp(m_old - m_new)
            p = jnp.exp(y - (m_new - cq))
            l_scr[hh] = alpha * l_scr[hh] + jnp.sum(p, axis=0, keepdims=True)
            m_scr[hh] = m_new
            ps.append(p.astype(BF16))
            alphas.append(alpha)
        for hh in range(hpg):
            vt = vt_ref[0, 0, hh * B_HEAD_DIM:(hh + 1) * B_HEAD_DIM, ks]
            acc_scr[hh] = alphas[hh] * acc_scr[hh] + jnp.dot(vt, ps[hh], preferred_element_type=F32)

    def body(kb, carry):
        step(kb, None)
        return carry

    lax.fori_loop(0, qi * nsub, body, 0)
    for j in range(nsub):
        step(qi * nsub + j, j * tk)
    ot = jnp.concatenate([acc_scr[hh] / l_scr[hh] for hh in range(hpg)], axis=0)
    o_ref[...] = ot.T.astype(o_ref.dtype)


def _fox_prefill(qt, kb, vbt, ccol, crow, *, batch, seq, tq, tk, hpg=4):
    m = batch * seq
    nq = seq // tq
    ngrp = B_HEADS // hpg
    gw = hpg * B_HEAD_DIM
    vbt4 = vbt.reshape(batch, ngrp, gw, seq)
    crow4 = crow.reshape(batch, ngrp, hpg, seq)
    return pl.pallas_call(
        functools.partial(_fox_prefill_kernel, tq=tq, tk=tk, hpg=hpg),
        grid=(batch, ngrp, nq),
        in_specs=[
            pl.BlockSpec((gw, tq), lambda b, p, i: (p, b * nq + i)),
            pl.BlockSpec((seq, gw), lambda b, p, i: (b, p)),
            pl.BlockSpec((1, 1, gw, seq), lambda b, p, i: (b, p, 0, 0)),
            pl.BlockSpec((1, seq, B_HEADS), lambda b, p, i: (b, 0, 0)),
            pl.BlockSpec((1, 1, hpg, tq), lambda b, p, i: (b, p, 0, i)),
        ],
        out_specs=pl.BlockSpec((tq, gw), lambda b, p, i: (b * nq + i, p)),
        out_shape=jax.ShapeDtypeStruct((m, B_HEADS * B_HEAD_DIM), BF16),
        scratch_shapes=[
            pltpu.VMEM((hpg, 1, tq), F32),
            pltpu.VMEM((hpg, 1, tq), F32),
            pltpu.VMEM((hpg, B_HEAD_DIM, tq), F32),
        ],
        compiler_params=_params("parallel", "parallel", "arbitrary"),
        name="fox_prefill",
    )(qt, kb, vbt4, ccol, crow4)


def _cumsum_sample_kernel(*refs, n_steps, ppb):
    page_refs = refs[1:1 + ppb]
    new_ref, past_ref, newrow_ref, carry_scr = refs[1 + ppb:]
    j = pl.program_id(1)
    h = carry_scr.shape[0]

    @pl.when(j == 0)
    def _():
        carry_scr[...] = jnp.zeros_like(carry_scr)

    def local_cumsum(x):
        t = x.shape[1]
        upper = (lax.broadcasted_iota(jnp.int32, (t, PAGE_SIZE), 0)
                 <= lax.broadcasted_iota(jnp.int32, (t, PAGE_SIZE), 1))
        return _tri_matmul(x, upper, tri_on_left=False)

    loc = local_cumsum(jnp.concatenate([r[0] for r in page_refs], axis=0))
    carry = carry_scr[...]
    for jj in range(ppb):
        page = loc[jj * h:(jj + 1) * h]
        past_ref[0, :, jj * PAGE_SIZE:(jj + 1) * PAGE_SIZE] = page + carry
        carry = carry + page[:, PAGE_SIZE - 1:PAGE_SIZE]
    carry_scr[...] = carry

    @pl.when(j == n_steps - 1)
    def _():
        newrow_ref[0] = local_cumsum(new_ref[0]) + carry


def _cumsum_sample(page_table_flat, cache_lft, lft_new, *, batch, n_pages, dec_seq, ppb=16):
    h = cache_lft.shape[1]
    n_steps = n_pages // ppb

    def page(jj):
        return lambda b, j, pt: (pt[b * n_pages + j * ppb + jj], 0, 0)

    grid_spec = pltpu.PrefetchScalarGridSpec(
        num_scalar_prefetch=1,
        grid=(batch, n_steps),
        in_specs=[pl.BlockSpec((1, h, PAGE_SIZE), page(jj)) for jj in range(ppb)]
        + [pl.BlockSpec((1, h, dec_seq), lambda b, j, pt: (b, 0, 0))],
        out_specs=[
            pl.BlockSpec((1, h, ppb * PAGE_SIZE), lambda b, j, pt: (b, 0, j)),
            pl.BlockSpec((1, h, PAGE_SIZE), lambda b, j, pt: (b, 0, 0)),
        ],
        scratch_shapes=[pltpu.VMEM((h, 1), F32)],
    )
    return pl.pallas_call(
        functools.partial(_cumsum_sample_kernel, n_steps=n_steps, ppb=ppb),
        grid_spec=grid_spec,
        out_shape=[
            jax.ShapeDtypeStruct((batch, h, n_pages * PAGE_SIZE), F32),
            jax.ShapeDtypeStruct((batch, h, PAGE_SIZE), F32),
        ],
        compiler_params=_params("parallel", "arbitrary"),
        name="cumsum_sample",
    )(page_table_flat, *([cache_lft] * ppb), lft_new)


def _fox_decode_kernel(*refs, ppb, dec_seq, n_steps):
    hq_ref, qg_ref = refs[1:3]
    k_refs = refs[3:3 + ppb]
    v_refs = refs[3 + ppb:3 + 2 * ppb]
    ck_ref, ckn_ref, cq_ref, kn_ref, vn_ref, o_ref = refs[3 + 2 * ppb:9 + 2 * ppb]
    qbd_scr, m_scr, l_scr, acc_scr, bias_scr, knp_scr, vnp_scr = refs[9 + 2 * ppb:]
    s_id = pl.program_id(1)
    hd = B_HEADS * B_HEAD_DIM
    nrow = B_HEADS * dec_seq
    lane_head = lax.broadcasted_iota(jnp.int32, (1, hd), 1) // B_HEAD_DIM

    @pl.when(s_id == 0)
    def _():
        qn = _head_rms(hq_ref[...], qg_ref[...]) * (B_HEAD_DIM ** -0.5)
        for h2 in range(B_HEADS // 2):
            a = jnp.where(lane_head == 2 * h2, qn, 0.0)
            b = jnp.where(lane_head == 2 * h2 + 1, qn, 0.0)
            qbd_scr[2 * h2 * dec_seq:(2 * h2 + 2) * dec_seq, :] = jnp.concatenate([a, b], axis=0).astype(BF16)
        m_scr[...] = jnp.full_like(m_scr, -jnp.inf)
        l_scr[...] = jnp.zeros_like(l_scr)
        acc_scr[...] = jnp.zeros_like(acc_scr)
        knp_scr[...] = jnp.zeros_like(knp_scr)
        vnp_scr[...] = jnp.zeros_like(vnp_scr)
        knp_scr[:, 0:dec_seq] = kn_ref[0]
        vnp_scr[:, 0:dec_seq] = vn_ref[0]

    cq = cq_ref[0]

    def attend(kt_tiles, vt_tiles, ck_rows, mask):
        n = len(kt_tiles)
        qbd = qbd_scr[...]
        groups = [list(range(g, min(g + 2, n))) for g in range(0, n, 2)]
        s = jnp.concatenate(
            [jnp.dot(qbd, jnp.concatenate([kt_tiles[t] for t in g], axis=1), preferred_element_type=F32) for g in groups],
            axis=1)
        for h in range(B_HEADS):
            rs = slice(h * dec_seq, (h + 1) * dec_seq)
            bias_scr[rs, 0:n * PAGE_SIZE] = cq[rs] - ck_rows[h:h + 1, :]
        s = s + bias_scr[:, 0:n * PAGE_SIZE]
        if mask is not None:
            s = jnp.where(mask, s, MASK_VALUE)
        m_old = m_scr[...]
        m_new = jnp.maximum(m_old, jnp.max(s, axis=-1, keepdims=True))
        alpha = jnp.exp(m_old - m_new)
        p = jnp.exp(s - m_new)
        l_scr[...] = alpha * l_scr[...] + jnp.sum(p, axis=-1, keepdims=True)
        pv = None
        for g in groups:
            t = lax.dot_general(p[:, g[0] * PAGE_SIZE:(g[-1] + 1) * PAGE_SIZE].astype(BF16),
                                jnp.concatenate([vt_tiles[t] for t in g], axis=1), _NT, preferred_element_type=F32)
            pv = t if pv is None else pv + t
        acc_scr[...] = alpha * acc_scr[...] + pv
        m_scr[...] = m_new

    attend([r[0].reshape(hd, PAGE_SIZE).astype(BF16) for r in k_refs],
           [r[0].reshape(hd, PAGE_SIZE).astype(BF16) for r in v_refs], ck_ref[0], None)

    @pl.when(s_id == n_steps - 1)
    def _():
        r = lax.broadcasted_iota(jnp.int32, (nrow, PAGE_SIZE), 0) % dec_seq
        c = lax.broadcasted_iota(jnp.int32, (nrow, PAGE_SIZE), 1)
        attend([knp_scr[...].astype(BF16)], [vnp_scr[...].astype(BF16)], ckn_ref[0], c <= r)
        o = acc_scr[...] / l_scr[...]
        out = jnp.zeros((dec_seq, hd), F32)
        for h in range(B_HEADS):
            out = out + jnp.where(lane_head == h, o[h * dec_seq:(h + 1) * dec_seq, :], 0.0)
        o_ref[...] = out.astype(o_ref.dtype)


def _fox_decode(page_table_flat, hq, qg_full, cache_kt, cache_vt, ck_past, ck_new, cq_col, kt_new, vt_new,
                *, batch, n_pages, dec_seq, ppb):
    hd = B_HEADS * B_HEAD_DIM
    n_steps = n_pages // ppb
    nrow = B_HEADS * dec_seq

    def page(jj):
        return lambda b, s, pt: (pt[b * n_pages + s * ppb + jj], 0, 0, 0)

    page_block = (1, B_HEADS, B_HEAD_DIM, PAGE_SIZE)
    in_specs = [
        pl.BlockSpec((dec_seq, hd), lambda b, s, pt: (b, 0)),
        pl.BlockSpec((1, hd), lambda b, s, pt: (0, 0)),
    ]
    in_specs += [pl.BlockSpec(page_block, page(jj)) for jj in range(ppb)]
    in_specs += [pl.BlockSpec(page_block, page(jj)) for jj in range(ppb)]
    in_specs += [
        pl.BlockSpec((1, B_HEADS, ppb * PAGE_SIZE), lambda b, s, pt: (b, 0, s)),
        pl.BlockSpec((1, B_HEADS, PAGE_SIZE), lambda b, s, pt: (b, 0, 0)),
        pl.BlockSpec((1, nrow, 1), lambda b, s, pt: (b, 0, 0)),
        pl.BlockSpec((1, hd, dec_seq), lambda b, s, pt: (b, 0, 0)),
        pl.BlockSpec((1, hd, dec_seq), lambda b, s, pt: (b, 0, 0)),
    ]
    grid_spec = pltpu.PrefetchScalarGridSpec(
        num_scalar_prefetch=1,
        grid=(batch, n_steps),
        in_specs=in_specs,
        out_specs=pl.BlockSpec((dec_seq, hd), lambda b, s, pt: (b, 0)),
        scratch_shapes=[
            pltpu.VMEM((nrow, hd), BF16),
            pltpu.VMEM((nrow, 1), F32),
            pltpu.VMEM((nrow, 1), F32),
            pltpu.VMEM((nrow, hd), F32),
            pltpu.VMEM((nrow, ppb * PAGE_SIZE), F32),
            pltpu.VMEM((hd, PAGE_SIZE), F32),
            pltpu.VMEM((hd, PAGE_SIZE), F32),
        ],
    )
    args = ([page_table_flat, hq, qg_full] + [cache_kt] * ppb + [cache_vt] * ppb
            + [ck_past, ck_new, cq_col, kt_new, vt_new])
    return pl.pallas_call(
        functools.partial(_fox_decode_kernel, ppb=ppb, dec_seq=dec_seq, n_steps=n_steps),
        grid_spec=grid_spec,
        out_shape=jax.ShapeDtypeStruct((batch * dec_seq, hd), F32),
        compiler_params=_params("parallel", "arbitrary"),
        name="fox_decode",
    )(*args)


def _trunk(x, batch, seq, states, w, attention, *, tm, kv_tm, kv_rows, gla_cfg):
    n_a = w["a_w_in"].shape[0]
    depth = w["wgu"].shape[0]
    new_states = []
    kv = ctx = None
    for layer in range(depth):
        if layer < n_a:
            h = _norm_matmul(x, w["a_norm"][layer:layer + 1], w["a_w_in"], layer, tm=tm, tn=1024)
            s0 = None if states is None else states[layer]
            og, s_new = _gla(h, w["a_lb"], w["a_onorm"][layer:layer + 1], s0, layer=layer, batch=batch, seq=seq, **gla_cfg)
            new_states.append(s_new)
            mix, wo3, wo_layer = og, w["a_w_out"], layer
        else:
            j = layer - n_a
            if j == 0:
                kv = _shared_kv(x, w["kv_norm"], w["wkt"], w["wvt"], w["wft"], w["bf"], w["kg"],
                                batch=kv_rows[0], seq=kv_rows[1], tm=kv_tm, with_rows=attention.with_rows)
                ctx = attention.prepare(kv)
            mix, wo3, wo_layer = attention.run(j, x, w, ctx, kv), w["b_wo"], j
        x = _ffn(mix, wo3, wo_layer, x, w["ffn_norm"][layer:layer + 1], w["wgu"], w["wd"], layer, tm=min(tm, 512))
    return x, jnp.stack(new_states), kv


class _PromptAttention:
    with_rows = False

    def __init__(self, batch, seq, tq, tk):
        self.batch, self.seq, self.tq, self.tk = batch, seq, tq, tk

    def prepare(self, kv):
        return _cumsum_prompt(kv[2])

    def run(self, j, x, w, ctx, kv):
        crow, ccol = ctx
        qt = _qproj_t(x, w["b_norm"][j:j + 1], w["b_wq_t"], j, w["qg_col"][j], tm=512)
        return _fox_prefill(qt, kv[3], kv[4], ccol, crow, batch=self.batch, seq=self.seq, tq=self.tq, tk=self.tk)


class _SampleAttention:
    with_rows = True

    def __init__(self, batch, dec_seq, n_pages, pt_flat, cache_kt, cache_vt, cache_lft, ppb):
        self.batch, self.dec_seq, self.n_pages, self.pt = batch, dec_seq, n_pages, pt_flat
        self.cache_kt, self.cache_vt, self.cache_lft, self.ppb = cache_kt, cache_vt, cache_lft, ppb

    def _per_batch(self, t):
        rows = t.shape[0]
        return t.reshape(rows, self.batch, self.dec_seq).transpose(1, 0, 2)

    def prepare(self, kv):
        hd = B_HEADS * B_HEAD_DIM
        lft_new = self._per_batch(kv[2][0])
        ck_past, ck_new = _cumsum_sample(self.pt, self.cache_lft, lft_new, batch=self.batch, n_pages=self.n_pages,
                                         dec_seq=self.dec_seq)
        cq_col = ck_new[:, :, :self.dec_seq].reshape(self.batch, B_HEADS * self.dec_seq, 1)
        kt_new = self._per_batch(kv[0].reshape(hd, -1))
        vt_new = self._per_batch(kv[1].reshape(hd, -1))
        return ck_past, ck_new, cq_col, kt_new, vt_new

    def run(self, j, x, w, ctx, kv):
        ck_past, ck_new, cq_col, kt_new, vt_new = ctx
        hq = _norm_matmul(x, w["b_norm"][j:j + 1], w["b_wq"], j, tm=x.shape[0], tn=1024)
        return _fox_decode(self.pt, hq, w["qg_full"][j:j + 1], self.cache_kt, self.cache_vt, ck_past, ck_new, cq_col,
                           kt_new, vt_new, batch=self.batch, n_pages=self.n_pages, dec_seq=self.dec_seq, ppb=self.ppb)


def kernel(x_prompt, x_sample, state_hgrn, cache_k, cache_v, cache_logf, page_table, a_norm, a_w_in, a_lb, a_onorm, a_w_out, kv_norm, w_kvf, b_f, k_norm, b_norm, b_wq, q_norm, b_wo, ffn_norm, w_gate_up, w_down):
    batch, seq, d = x_prompt.shape
    dec_batch, dec_seq, _ = x_sample.shape
    n_pages = page_table.shape[1]
    hd = B_HEADS * B_HEAD_DIM
    assert seq % A_CHUNK == 0 and dec_seq <= A_CHUNK

    w_kvf_t = w_kvf.T.astype(BF16)
    w = dict(
        a_norm=a_norm, a_w_in=a_w_in.astype(BF16), a_lb=a_lb.astype(F32), a_onorm=a_onorm,
        a_w_out=a_w_out.astype(BF16), kv_norm=kv_norm.reshape(1, d),
        wkt=w_kvf_t[:hd], wvt=w_kvf_t[hd:2 * hd], wft=w_kvf_t[2 * hd:],
        bf=b_f.astype(F32).reshape(B_HEADS, 1), kg=k_norm.astype(F32).reshape(B_HEAD_DIM, 1),
        b_norm=b_norm, b_wq=b_wq.astype(BF16), b_wo=b_wo.astype(BF16), ffn_norm=ffn_norm,
        wgu=w_gate_up.astype(BF16), wd=w_down.astype(BF16),
    )
    w["b_wq_t"] = jnp.swapaxes(b_wq, 1, 2).astype(BF16)
    w["qg_col"] = q_norm.astype(F32)[:, :, None]
    w["qg_full"] = jnp.tile(q_norm, (1, B_HEADS))

    prompt_att = _PromptAttention(batch, seq, tq=1024, tk=512)
    y_p, hgrn_p, kv_p = _trunk(
        x_prompt.reshape(batch * seq, d), batch, seq, None, w, prompt_att, tm=1024, kv_tm=512, kv_rows=(batch, seq),
        gla_cfg=dict(c=A_CHUNK, u=A_SUB, t=512, hps=8, out_dtype=BF16))
    k_p = kv_p[0].transpose(0, 3, 1, 2)
    v_p = kv_p[1].transpose(0, 3, 1, 2)
    lf_p = kv_p[2].transpose(0, 2, 1)

    m_s = dec_batch * dec_seq
    pt_flat = page_table.reshape(-1).astype(jnp.int32)
    sample_att = _SampleAttention(dec_batch, dec_seq, n_pages, pt_flat,
                                  cache_k.transpose(0, 2, 3, 1), cache_v.transpose(0, 2, 3, 1),
                                  cache_logf.transpose(0, 2, 1), ppb=16)
    y_s, hgrn_s, kv_s = _trunk(
        x_sample.reshape(m_s, d), dec_batch, dec_seq, state_hgrn, w, sample_att, tm=m_s, kv_tm=m_s, kv_rows=(1, m_s),
        gla_cfg=dict(c=dec_seq, u=min(A_SUB, dec_seq), t=dec_seq, hps=A_HEADS, out_dtype=F32))
    k_s = kv_s[5].reshape(dec_batch, dec_seq, B_HEADS, B_HEAD_DIM)
    v_s = kv_s[6].reshape(dec_batch, dec_seq, B_HEADS, B_HEAD_DIM)
    lf_s = kv_s[2][0].T.reshape(dec_batch, dec_seq, B_HEADS)

    return (y_p.reshape(batch, seq, d), y_s.reshape(dec_batch, dec_seq, d), hgrn_p, k_p, v_p, lf_p,
            hgrn_s, k_s, v_s, lf_s)
```

```python
import functools

import jax
import jax.numpy as jnp
from jax import lax
from jax.experimental import pallas as pl
from jax.experimental.pallas import tpu as pltpu

F32 = jnp.float32
BF16 = jnp.bfloat16
HIGHEST = lax.Precision.HIGHEST

D_MODEL = 1024
A_HEADS = 8
A_DK = 128
A_DV = 128
A_CHUNK = 64
A_SUB = 16
B_HEADS = 16
B_HEAD_DIM = 64
PAGE_SIZE = 128
RMS_EPS = 1e-6
MASK_VALUE = -1e30
MIN_F = 1e-30
LOG2E = 1.4426950408889634
LANES = 128
SUBLANES = 8
VMEM_LIMIT = 56 * 1024 * 1024

_NT = (((1,), (1,)), ((), ()))
_TN = (((0,), (0,)), ((), ()))


def _params(*sem):
    return pltpu.CompilerParams(dimension_semantics=sem, vmem_limit_bytes=VMEM_LIMIT)


def _rms(x, g):
    return x * lax.rsqrt(jnp.mean(x * x, axis=-1, keepdims=True) + RMS_EPS) * g


def _tri_matmul(x, tri, *, tri_on_left):
    hi = x.astype(BF16)
    r1 = x - hi.astype(F32)
    mid = r1.astype(BF16)
    lo = (r1 - mid.astype(F32)).astype(BF16)
    t = tri.astype(BF16)
    out = None
    for piece in (hi, mid, lo):
        d = jnp.dot(t, piece, preferred_element_type=F32) if tri_on_left else jnp.dot(piece, t, preferred_element_type=F32)
        out = d if out is None else out + d
    return out


def _resident(shape):
    return pl.BlockSpec(shape, lambda *_: (0,) * len(shape), pipeline_mode=pl.Buffered(1))


def _resident_layer(w3, layer, col_block=0, cols=None):
    _, r, c = w3.shape
    return pl.BlockSpec((None, r, cols or c), lambda *_: (layer, 0, col_block), pipeline_mode=pl.Buffered(1))


def _group_indicator(n, groups_padded, group):
    r = lax.broadcasted_iota(jnp.int32, (n, groups_padded), 0)
    c = lax.broadcasted_iota(jnp.int32, (n, groups_padded), 1)
    return (r // group == c).astype(F32)


def _head_rms(x, gain_tiled):
    n = x.shape[-1]
    g = _group_indicator(n, LANES, B_HEAD_DIM)
    ss = jnp.dot(x * x, g, preferred_element_type=F32, precision=HIGHEST)
    rs = lax.rsqrt(ss * (1.0 / B_HEAD_DIM) + RMS_EPS)
    rs_full = lax.dot_general(rs, g, _NT, preferred_element_type=F32, precision=HIGHEST)
    return x * rs_full * gain_tiled


def _norm_matmul_kernel(x_ref, g_ref, w_ref, o_ref, xn_ref):
    @pl.when(pl.program_id(1) == 0)
    def _():
        xn_ref[...] = _rms(x_ref[...], g_ref[...]).astype(BF16)

    o_ref[...] = jnp.dot(xn_ref[...], w_ref[...], preferred_element_type=F32).astype(o_ref.dtype)


def _norm_matmul(x, g, w3, layer, *, tm, tn, out_dtype=F32):
    m, k = x.shape
    n = w3.shape[2]
    return pl.pallas_call(
        _norm_matmul_kernel,
        grid=(m // tm, n // tn),
        in_specs=[
            pl.BlockSpec((tm, k), lambda i, j: (i, 0)),
            pl.BlockSpec((1, k), lambda i, j: (0, 0)),
            pl.BlockSpec((None, k, tn), lambda i, j: (layer, 0, j)),
        ],
        out_specs=pl.BlockSpec((tm, tn), lambda i, j: (i, j)),
        out_shape=jax.ShapeDtypeStruct((m, n), out_dtype),
        scratch_shapes=[pltpu.VMEM((tm, k), BF16)],
        compiler_params=_params("parallel", "arbitrary"),
        name="norm_matmul",
    )(x, g, w3)


def _ffn_kernel(a_ref, wo_ref, r_ref, g_ref, wg_ref, wu_ref, wd_ref, o_ref, h_ref, act_ref, *, tf):
    x = r_ref[...] + jnp.dot(a_ref[...].astype(BF16), wo_ref[...], preferred_element_type=F32)
    h_ref[...] = _rms(x, g_ref[...]).astype(BF16)
    for j in range(act_ref.shape[1] // tf):
        sl = slice(j * tf, (j + 1) * tf)
        a = jnp.dot(h_ref[...], wg_ref[:, sl], preferred_element_type=F32)
        b = jnp.dot(h_ref[...], wu_ref[:, sl], preferred_element_type=F32)
        act_ref[:, sl] = (jax.nn.silu(a) * b).astype(BF16)
    o_ref[...] = x + jnp.dot(act_ref[...], wd_ref[...], preferred_element_type=F32)


def _ffn(a, wo3, wo_layer, res, g, wgu3, wd3, layer, *, tm, tf=256):
    m, d = res.shape
    dff = wd3.shape[1]
    return pl.pallas_call(
        functools.partial(_ffn_kernel, tf=tf),
        grid=(m // tm,),
        in_specs=[
            pl.BlockSpec((tm, a.shape[1]), lambda i: (i, 0)),
            _resident_layer(wo3, wo_layer),
            pl.BlockSpec((tm, d), lambda i: (i, 0)),
            pl.BlockSpec((1, d), lambda i: (0, 0)),
            _resident_layer(wgu3, layer, 0, dff),
            _resident_layer(wgu3, layer, 1, dff),
            _resident_layer(wd3, layer),
        ],
        out_specs=pl.BlockSpec((tm, d), lambda i: (i, 0)),
        out_shape=jax.ShapeDtypeStruct((m, d), F32),
        scratch_shapes=[pltpu.VMEM((tm, d), BF16), pltpu.VMEM((tm, dff), BF16)],
        compiler_params=_params("parallel"),
        name="ffn",
    )(a, wo3, res, g, wgu3, wgu3, wd3)


def _gla_gates(z, lb, c):
    sig = jax.nn.sigmoid(z)
    f = lb + (1.0 - lb) * sig
    logf = jnp.log(jnp.maximum(f, MIN_F))
    k = (1.0 - lb) * jax.nn.sigmoid(-z)
    tri = lax.broadcasted_iota(jnp.int32, (c, c), 0) >= lax.broadcasted_iota(jnp.int32, (c, c), 1)
    return k, _tri_matmul(logf, tri, tri_on_left=True)


def _gla_diag(q, b2, b_scr, k_scr, v_scr, *, c, u):
    rows = lax.broadcasted_iota(jnp.int32, (SUBLANES, 1), 0)
    parts = []
    for i in range(c // u):
        lo = u * i
        for j in range(u // SUBLANES):
            p0 = lo + j * SUBLANES
            qp = q[p0:p0 + SUBLANES]
            bp = b2[p0:p0 + SUBLANES]
            acc = jnp.zeros((SUBLANES, A_DV), F32)
            for s in range(lo, p0 + SUBLANES):
                d = bp - b_scr[s:s + 1, :]
                if s >= p0:
                    d = jnp.minimum(d, 0.0)
                a = jnp.sum(qp * k_scr[s:s + 1, :] * jnp.exp2(d), axis=-1, keepdims=True)
                if s >= p0:
                    a = jnp.where(rows >= s - p0, a, 0.0)
                acc = acc + a * v_scr[s:s + 1, :]
            parts.append(acc)
    return parts


def _gla_mxu(q, k, b, v, st, parts, *, c, u):
    vb = v.astype(BF16)
    o_state = lax.dot_general((q * jnp.exp(b)).astype(BF16), st.astype(BF16), _NT, preferred_element_type=F32)
    parts = list(parts)
    for i in range(1, c // u):
        lo = u * i
        r = b[lo - 1:lo, :]
        qt = (q[lo:lo + u] * jnp.exp(b[lo:lo + u] - r)).astype(BF16)
        kt = (k[:lo] * jnp.exp(r - b[:lo])).astype(BF16)
        att = lax.dot_general(qt, kt, _NT, preferred_element_type=F32)
        off = jnp.dot(att.astype(BF16), vb[:lo], preferred_element_type=F32)
        for j in range(u // SUBLANES):
            idx = (lo + j * SUBLANES) // SUBLANES
            parts[idx] = parts[idx] + off[j * SUBLANES:(j + 1) * SUBLANES]
    o = (parts[0] if len(parts) == 1 else jnp.concatenate(parts, axis=0)) + o_state
    bl = b[c - 1:c, :]
    kd = (k * jnp.exp(bl - b)).astype(BF16)
    st_new = st * jnp.exp(bl) + lax.dot_general(vb, kd, _TN, preferred_element_type=F32)
    return o, st_new


def _gla_kernel(*refs, layer, c, u, n_chunks, hps, has_s0):
    if has_s0:
        q_ref, z_ref, v_ref, gt_ref, lb_ref, og_ref, s0_ref, o_ref, so_ref, st_ref, b_scr, k_scr, v_scr = refs
    else:
        q_ref, z_ref, v_ref, gt_ref, lb_ref, og_ref, o_ref, so_ref, st_ref, b_scr, k_scr, v_scr = refs
        s0_ref = None
    i = pl.program_id(2)

    @pl.when(i == 0)
    def _():
        for hh in range(hps):
            st_ref[hh] = s0_ref[0, hh].T if has_s0 else jnp.zeros((A_DV, A_DK), F32)

    a = lb_ref[...]
    e = jnp.exp(a - jnp.max(a, axis=0, keepdims=True))
    p = e / jnp.sum(e, axis=0, keepdims=True)
    lb_all = jnp.sum(p[:layer + 1], axis=0, keepdims=True) - p[0:1]
    og = og_ref[...]

    def body(ci, carry):
        rs = pl.ds(pl.multiple_of(ci * c, c), c)
        cols = [slice(hh * LANES, (hh + 1) * LANES) for hh in range(hps)]
        kb = [_gla_gates(z_ref[rs, cs], lb_all[:, cs], c) for cs in cols]
        for hh, cs in enumerate(cols):
            b_scr[hh] = kb[hh][1] * LOG2E
            k_scr[hh] = kb[hh][0]
            v_scr[hh] = v_ref[rs, cs]
        parts = [_gla_diag(q_ref[rs, cs], b_scr[hh], b_scr.at[hh], k_scr.at[hh], v_scr.at[hh], c=c, u=u)
                 for hh, cs in enumerate(cols)]
        for hh, cs in enumerate(cols):
            o, st_new = _gla_mxu(q_ref[rs, cs], kb[hh][0], kb[hh][1], v_ref[rs, cs], st_ref[hh], parts[hh], c=c, u=u)
            st_ref[hh] = st_new
            on = o * lax.rsqrt(jnp.mean(o * o, axis=-1, keepdims=True) + RMS_EPS) * og
            o_ref[rs, cs] = (on * jax.nn.silu(gt_ref[rs, cs])).astype(o_ref.dtype)
        return carry

    if n_chunks == 1:
        body(0, 0)
    else:
        lax.fori_loop(0, n_chunks, body, 0)

    @pl.when(i == pl.num_programs(2) - 1)
    def _():
        for hh in range(hps):
            so_ref[0, hh] = st_ref[hh].T


def _gla(h, a_lb, o_gain, s0, *, layer, batch, seq, c, u, t, hps, out_dtype):
    m = batch * seq
    nt = seq // t
    hb = A_HEADS // hps
    w = hps * LANES
    kd = A_HEADS * A_DK
    nblk = kd // w

    def col(off):
        return lambda b, hg, i: (b * nt + i, off * nblk + hg)

    in_specs = [
        pl.BlockSpec((t, w), col(0)),
        pl.BlockSpec((t, w), col(1)),
        pl.BlockSpec((t, w), col(2)),
        pl.BlockSpec((t, w), col(3)),
        pl.BlockSpec((a_lb.shape[0], w), lambda b, hg, i: (0, hg)),
        pl.BlockSpec((1, A_DV), lambda b, hg, i: (0, 0)),
    ]
    args = [h, h, h, h, a_lb, o_gain]
    if s0 is not None:
        in_specs.append(pl.BlockSpec((1, hps, A_DK, A_DV), lambda b, hg, i: (b, hg, 0, 0)))
        args.append(s0)
    kern = functools.partial(_gla_kernel, layer=layer, c=c, u=u, n_chunks=t // c, hps=hps, has_s0=s0 is not None)
    return pl.pallas_call(
        kern,
        grid=(batch, hb, nt),
        in_specs=in_specs,
        out_specs=[
            pl.BlockSpec((t, w), lambda b, hg, i: (b * nt + i, hg)),
            pl.BlockSpec((1, hps, A_DK, A_DV), lambda b, hg, i: (b, hg, 0, 0)),
        ],
        out_shape=[
            jax.ShapeDtypeStruct((m, kd), out_dtype),
            jax.ShapeDtypeStruct((batch, A_HEADS, A_DK, A_DV), F32),
        ],
        scratch_shapes=[
            pltpu.VMEM((hps, A_DV, A_DK), F32),
            pltpu.VMEM((hps, c, A_DK), F32),
            pltpu.VMEM((hps, c, A_DK), F32),
            pltpu.VMEM((hps, c, A_DV), F32),
        ],
        compiler_params=_params("parallel", "parallel", "arbitrary"),
        name="hgrn2_gla",
    )(*args)


def _kv_kernel(x_ref, g_ref, wkt_ref, wvt_ref, wft_ref, bf_ref, kg_ref, *out_refs, with_rows):
    kt_ref, vt_ref, lft_ref, kb_ref, vbt_ref = out_refs[:5]
    h = _rms(x_ref[...], g_ref[...]).astype(BF16)
    tm = h.shape[0]
    kt = lax.dot_general(wkt_ref[...], h, _NT, preferred_element_type=F32)
    k3 = kt.reshape(B_HEADS, B_HEAD_DIM, tm)
    ms = jnp.mean(k3 * k3, axis=1, keepdims=True)
    k3 = k3 * lax.rsqrt(ms + RMS_EPS) * kg_ref[...][None]
    vt = lax.dot_general(wvt_ref[...], h, _NT, preferred_element_type=F32)
    fz = lax.dot_general(wft_ref[...], h, _NT, preferred_element_type=F32) + bf_ref[...]
    krow = k3.reshape(B_HEADS * B_HEAD_DIM, tm).T
    kt_ref[0] = k3
    vt_ref[0] = vt.reshape(B_HEADS, B_HEAD_DIM, tm)
    lft_ref[0] = jnp.minimum(fz, 0.0) - jnp.log1p(jnp.exp(-jnp.abs(fz)))
    kb_ref[...] = krow.astype(BF16)
    vbt_ref[0] = vt.reshape(B_HEADS, B_HEAD_DIM, tm).astype(BF16)
    if with_rows:
        krow_ref, vrow_ref = out_refs[5:]
        krow_ref[...] = krow
        vrow_ref[...] = vt.T


def _shared_kv(x, g, wkt, wvt, wft, bf, kg, *, batch, seq, tm, with_rows):
    m, d = x.shape
    hd = B_HEADS * B_HEAD_DIM
    nt = seq // tm
    t_spec = pl.BlockSpec((1, B_HEADS, B_HEAD_DIM, tm), lambda b, i: (b, 0, 0, i))
    row_spec = pl.BlockSpec((tm, hd), lambda b, i: (b * nt + i, 0))
    t_shape = (batch, B_HEADS, B_HEAD_DIM, seq)
    out_specs = [t_spec, t_spec, pl.BlockSpec((1, B_HEADS, tm), lambda b, i: (b, 0, i)), row_spec, t_spec]
    out_shape = [
        jax.ShapeDtypeStruct(t_shape, F32),
        jax.ShapeDtypeStruct(t_shape, F32),
        jax.ShapeDtypeStruct((batch, B_HEADS, seq), F32),
        jax.ShapeDtypeStruct((m, hd), BF16),
        jax.ShapeDtypeStruct(t_shape, BF16),
    ]
    if with_rows:
        out_specs += [row_spec, row_spec]
        out_shape += [jax.ShapeDtypeStruct((m, hd), F32)] * 2
    return pl.pallas_call(
        functools.partial(_kv_kernel, with_rows=with_rows),
        grid=(batch, nt),
        in_specs=[
            pl.BlockSpec((tm, d), lambda b, i: (b * nt + i, 0)),
            pl.BlockSpec((1, d), lambda b, i: (0, 0)),
            _resident((hd, d)),
            _resident((hd, d)),
            _resident((B_HEADS, d)),
            pl.BlockSpec((B_HEADS, 1), lambda b, i: (0, 0)),
            pl.BlockSpec((B_HEAD_DIM, 1), lambda b, i: (0, 0)),
        ],
        out_specs=out_specs,
        out_shape=out_shape,
        compiler_params=_params("parallel", "parallel"),
        name="shared_kv",
    )(x, g, wkt, wvt, wft, bf, kg)


def _qproj_kernel(x_ref, g_ref, wt_ref, qg_ref, o_ref):
    h = _rms(x_ref[...], g_ref[...]).astype(BF16)
    tm = h.shape[0]
    qt = lax.dot_general(wt_ref[...], h, _NT, preferred_element_type=F32)
    q3 = qt.reshape(B_HEADS, B_HEAD_DIM, tm)
    ms = jnp.mean(q3 * q3, axis=1, keepdims=True)
    q3 = q3 * lax.rsqrt(ms + RMS_EPS) * qg_ref[...][None] * (B_HEAD_DIM ** -0.5)
    o_ref[...] = q3.reshape(B_HEADS * B_HEAD_DIM, tm).astype(BF16)


def _qproj_t(x, g, wt3, layer, qg, *, tm):
    m, d = x.shape
    hd = wt3.shape[1]
    return pl.pallas_call(
        _qproj_kernel,
        grid=(m // tm,),
        in_specs=[
            pl.BlockSpec((tm, d), lambda i: (i, 0)),
            pl.BlockSpec((1, d), lambda i: (0, 0)),
            _resident_layer(wt3, layer),
            pl.BlockSpec((B_HEAD_DIM, 1), lambda i: (0, 0)),
        ],
        out_specs=pl.BlockSpec((hd, tm), lambda i: (0, i)),
        out_shape=jax.ShapeDtypeStruct((hd, m), BF16),
        compiler_params=_params("parallel"),
        name="qproj_t",
    )(x, g, wt3, qg)


def _cumsum_prompt_kernel(x_ref, row_ref, col_ref, carry_scr):
    @pl.when(pl.program_id(1) == 0)
    def _():
        carry_scr[...] = jnp.zeros_like(carry_scr)

    x = x_ref[0]
    t = x.shape[1]
    upper = lax.broadcasted_iota(jnp.int32, (t, t), 0) <= lax.broadcasted_iota(jnp.int32, (t, t), 1)
    row = _tri_matmul(x, upper, tri_on_left=False) + carry_scr[...]
    row_ref[0] = row
    col_ref[0] = row.T
    carry_scr[...] = row[:, t - 1:t]


def _cumsum_prompt(lft, *, t=256):
    b, h, l = lft.shape
    return pl.pallas_call(
        _cumsum_prompt_kernel,
        grid=(b, l // t),
        in_specs=[pl.BlockSpec((1, h, t), lambda i, j: (i, 0, j))],
        out_specs=[
            pl.BlockSpec((1, h, t), lambda i, j: (i, 0, j)),
            pl.BlockSpec((1, t, h), lambda i, j: (i, j, 0)),
        ],
        out_shape=[jax.ShapeDtypeStruct((b, h, l), F32), jax.ShapeDtypeStruct((b, l, h), F32)],
        scratch_shapes=[pltpu.VMEM((h, 1), F32)],
        compiler_params=_params("parallel", "arbitrary"),
        name="cumsum_prompt",
    )(lft)


def _fox_prefill_kernel(qt_ref, k_ref, vt_ref, ccol_ref, crow_ref, o_ref, m_scr, l_scr, acc_scr, *, tq, tk, hpg):
    grp = pl.program_id(1)
    qi = pl.program_id(2)
    row_head = lax.broadcasted_iota(jnp.int32, (2 * B_HEAD_DIM, 1), 0) // B_HEAD_DIM
    qm = []
    for pr in range(hpg // 2):
        qt = qt_ref[pr * 2 * B_HEAD_DIM:(pr + 1) * 2 * B_HEAD_DIM, :]
        qm += [jnp.where(row_head == hh, qt, jnp.zeros_like(qt)) for hh in range(2)]
    hl = lax.broadcasted_iota(jnp.int32, (1, B_HEADS), 1)
    nsub = tq // tk

    m_scr[...] = jnp.full_like(m_scr, -jnp.inf)
    l_scr[...] = jnp.zeros_like(l_scr)
    acc_scr[...] = jnp.zeros_like(acc_scr)

    def step(kb, diag):
        ks = pl.ds(pl.multiple_of(kb * tk, tk), tk)
        ccol = ccol_ref[0, ks, :]
        q0 = diag or 0
        qs = slice(q0, tq)
        ys = []
        for hh in range(hpg):
            pr = hh // 2
            k = k_ref[ks, pr * 2 * B_HEAD_DIM:(pr + 1) * 2 * B_HEAD_DIM]
            ys.append(jnp.dot(k, qm[hh][:, qs], preferred_element_type=F32))
        ps = []
        alphas = []
        for hh in range(hpg):
            ck = jnp.sum(jnp.where(hl == hpg * grp + hh, ccol, 0.0), axis=-1, keepdims=True)
            y = ys[hh] - ck
            if diag is not None:
                r = lax.broadcasted_iota(jnp.int32, (tk, tq - q0), 0)
                c = lax.broadcasted_iota(jnp.int32, (tk, tq - q0), 1)
                y = jnp.where(r <= c, y, MASK_VALUE)
            cq = crow_ref[0, 0, hh:hh + 1, qs]
            m_old = m_scr[hh, :, qs]
            m_new = jnp.maximum(m_old, jnp.max(y, axis=0, keepdims=True) + cq)
            alpha = jnp.exp(m_old - m_new)
            p = jnp.exp(y - (m_new - cq))
            l_scr[hh, :, qs] = alpha * l_scr[hh, :, qs] + jnp.sum(p, axis=0, keepdims=True)
            m_scr[hh, :, qs] = m_new
            ps.append(p.astype(BF16))
            alphas.append(alpha)
        for hh in range(hpg):
            vt = vt_ref[0, 0, hh * B_HEAD_DIM:(hh + 1) * B_HEAD_DIM, ks]
            acc_scr[hh, :, qs] = alphas[hh] * acc_scr[hh, :, qs] + jnp.dot(vt, ps[hh], preferred_element_type=F32)

    def body(kb, carry):
        step(kb, None)
        return carry

    lax.fori_loop(0, qi * nsub, body, 0)
    for j in range(nsub):
        step(qi * nsub + j, j * tk)
    ot = jnp.concatenate([acc_scr[hh] / l_scr[hh] for hh in range(hpg)], axis=0)
    o_ref[...] = ot.T.astype(o_ref.dtype)


def _fox_prefill(qt, kb, vbt, ccol, crow, *, batch, seq, tq, tk, hpg=4):
    m = batch * seq
    nq = seq // tq
    ngrp = B_HEADS // hpg
    gw = hpg * B_HEAD_DIM
    vbt4 = vbt.reshape(batch, ngrp, gw, seq)
    crow4 = crow.reshape(batch, ngrp, hpg, seq)
    return pl.pallas_call(
        functools.partial(_fox_prefill_kernel, tq=tq, tk=tk, hpg=hpg),
        grid=(batch, ngrp, nq),
        in_specs=[
            pl.BlockSpec((gw, tq), lambda b, p, i: (p, b * nq + i)),
            pl.BlockSpec((seq, gw), lambda b, p, i: (b, p)),
            pl.BlockSpec((1, 1, gw, seq), lambda b, p, i: (b, p, 0, 0)),
            pl.BlockSpec((1, seq, B_HEADS), lambda b, p, i: (b, 0, 0)),
            pl.BlockSpec((1, 1, hpg, tq), lambda b, p, i: (b, p, 0, i)),
        ],
        out_specs=pl.BlockSpec((tq, gw), lambda b, p, i: (b * nq + i, p)),
        out_shape=jax.ShapeDtypeStruct((m, B_HEADS * B_HEAD_DIM), BF16),
        scratch_shapes=[
            pltpu.VMEM((hpg, 1, tq), F32),
            pltpu.VMEM((hpg, 1, tq), F32),
            pltpu.VMEM((hpg, B_HEAD_DIM, tq), F32),
        ],
        compiler_params=_params("parallel", "parallel", "arbitrary"),
        name="fox_prefill",
    )(qt, kb, vbt4, ccol, crow4)


def _cumsum_sample_kernel(*refs, n_steps, ppb):
    page_refs = refs[1:1 + ppb]
    new_ref, past_ref, newrow_ref, carry_scr = refs[1 + ppb:]
    j = pl.program_id(1)
    h = carry_scr.shape[0]

    @pl.when(j == 0)
    def _():
        carry_scr[...] = jnp.zeros_like(carry_scr)

    def local_cumsum(x):
        t = x.shape[1]
        upper = (lax.broadcasted_iota(jnp.int32, (t, PAGE_SIZE), 0)
                 <= lax.broadcasted_iota(jnp.int32, (t, PAGE_SIZE), 1))
        return _tri_matmul(x, upper, tri_on_left=False)

    loc = local_cumsum(jnp.concatenate([r[0] for r in page_refs], axis=0))
    carry = carry_scr[...]
    for jj in range(ppb):
        page = loc[jj * h:(jj + 1) * h]
        past_ref[0, :, jj * PAGE_SIZE:(jj + 1) * PAGE_SIZE] = page + carry
        carry = carry + page[:, PAGE_SIZE - 1:PAGE_SIZE]
    carry_scr[...] = carry

    @pl.when(j == n_steps - 1)
    def _():
        newrow_ref[0] = local_cumsum(new_ref[0]) + carry


def _cumsum_sample(page_table_flat, cache_lft, lft_new, *, batch, n_pages, dec_seq, ppb=32):
    h = cache_lft.shape[1]
    ppb = min(ppb, n_pages)
    n_steps = n_pages // ppb

    def page(jj):
        return lambda b, j, pt: (pt[b * n_pages + j * ppb + jj], 0, 0)

    grid_spec = pltpu.PrefetchScalarGridSpec(
        num_scalar_prefetch=1,
        grid=(batch, n_steps),
        in_specs=[pl.BlockSpec((1, h, PAGE_SIZE), page(jj)) for jj in range(ppb)]
        + [pl.BlockSpec((1, h, dec_seq), lambda b, j, pt: (b, 0, 0))],
        out_specs=[
            pl.BlockSpec((1, h, ppb * PAGE_SIZE), lambda b, j, pt: (b, 0, j)),
            pl.BlockSpec((1, h, PAGE_SIZE), lambda b, j, pt: (b, 0, 0)),
        ],
        scratch_shapes=[pltpu.VMEM((h, 1), F32)],
    )
    return pl.pallas_call(
        functools.partial(_cumsum_sample_kernel, n_steps=n_steps, ppb=ppb),
        grid_spec=grid_spec,
        out_shape=[
            jax.ShapeDtypeStruct((batch, h, n_pages * PAGE_SIZE), F32),
            jax.ShapeDtypeStruct((batch, h, PAGE_SIZE), F32),
        ],
        compiler_params=_params("parallel", "arbitrary"),
        name="cumsum_sample",
    )(page_table_flat, *([cache_lft] * ppb), lft_new)


def _fox_decode_kernel(*refs, ppb, dec_seq, n_steps):
    hq_ref, qg_ref = refs[1:3]
    k_refs = refs[3:3 + ppb]
    v_refs = refs[3 + ppb:3 + 2 * ppb]
    ck_ref, ckn_ref, cq_ref, kn_ref, vn_ref, o_ref = refs[3 + 2 * ppb:9 + 2 * ppb]
    qbd_scr, m_scr, l_scr, acc_scr, bias_scr, knp_scr, vnp_scr = refs[9 + 2 * ppb:]
    s_id = pl.program_id(1)
    hd = B_HEADS * B_HEAD_DIM
    nrow = B_HEADS * dec_seq
    lane_head = lax.broadcasted_iota(jnp.int32, (1, hd), 1) // B_HEAD_DIM

    @pl.when(s_id == 0)
    def _():
        qn = _head_rms(hq_ref[...], qg_ref[...]) * (B_HEAD_DIM ** -0.5)
        for h2 in range(B_HEADS // 2):
            a = jnp.where(lane_head == 2 * h2, qn, 0.0)
            b = jnp.where(lane_head == 2 * h2 + 1, qn, 0.0)
            qbd_scr[2 * h2 * dec_seq:(2 * h2 + 2) * dec_seq, :] = jnp.concatenate([a, b], axis=0).astype(BF16)
        m_scr[...] = jnp.full_like(m_scr, -jnp.inf)
        l_scr[...] = jnp.zeros_like(l_scr)
        acc_scr[...] = jnp.zeros_like(acc_scr)
        knp_scr[...] = jnp.zeros_like(knp_scr)
        vnp_scr[...] = jnp.zeros_like(vnp_scr)
        knp_scr[:, 0:dec_seq] = kn_ref[0]
        vnp_scr[:, 0:dec_seq] = vn_ref[0]

    cq = cq_ref[0]

    def attend(kt_tiles, vt_tiles, ck_rows, mask):
        n = len(kt_tiles)
        qbd = qbd_scr[...]
        groups = [list(range(g, min(g + 2, n))) for g in range(0, n, 2)]
        s = jnp.concatenate(
            [jnp.dot(qbd, jnp.concatenate([kt_tiles[t] for t in g], axis=1), preferred_element_type=F32) for g in groups],
            axis=1)
        for h in range(B_HEADS):
            rs = slice(h * dec_seq, (h + 1) * dec_seq)
            bias_scr[rs, 0:n * PAGE_SIZE] = cq[rs] - ck_rows[h:h + 1, :]
        s = s + bias_scr[:, 0:n * PAGE_SIZE]
        if mask is not None:
            s = jnp.where(mask, s, MASK_VALUE)
        m_old = m_scr[...]
        m_new = jnp.maximum(m_old, jnp.max(s, axis=-1, keepdims=True))
        alpha = jnp.exp(m_old - m_new)
        p = jnp.exp(s - m_new)
        l_scr[...] = alpha * l_scr[...] + jnp.sum(p, axis=-1, keepdims=True)
        pv = None
        for g in groups:
            t = lax.dot_general(p[:, g[0] * PAGE_SIZE:(g[-1] + 1) * PAGE_SIZE].astype(BF16),
                                jnp.concatenate([vt_tiles[t] for t in g], axis=1), _NT, preferred_element_type=F32)
            pv = t if pv is None else pv + t
        acc_scr[...] = alpha * acc_scr[...] + pv
        m_scr[...] = m_new

    attend([r[0].reshape(hd, PAGE_SIZE).astype(BF16) for r in k_refs],
           [r[0].reshape(hd, PAGE_SIZE).astype(BF16) for r in v_refs], ck_ref[0], None)

    @pl.when(s_id == n_steps - 1)
    def _():
        r = lax.broadcasted_iota(jnp.int32, (nrow, PAGE_SIZE), 0) % dec_seq
        c = lax.broadcasted_iota(jnp.int32, (nrow, PAGE_SIZE), 1)
        attend([knp_scr[...].astype(BF16)], [vnp_scr[...].astype(BF16)], ckn_ref[0], c <= r)
        o = acc_scr[...] / l_scr[...]
        out = jnp.zeros((dec_seq, hd), F32)
        for h in range(B_HEADS):
            out = out + jnp.where(lane_head == h, o[h * dec_seq:(h + 1) * dec_seq, :], 0.0)
        o_ref[...] = out.astype(o_ref.dtype)


def _fox_decode(page_table_flat, hq, qg_full, cache_kt, cache_vt, ck_past, ck_new, cq_col, kt_new, vt_new,
                *, batch, n_pages, dec_seq, ppb):
    hd = B_HEADS * B_HEAD_DIM
    n_steps = n_pages // ppb
    nrow = B_HEADS * dec_seq

    def page(jj):
        return lambda b, s, pt: (pt[b * n_pages + s * ppb + jj], 0, 0, 0)

    page_block = (1, B_HEADS, B_HEAD_DIM, PAGE_SIZE)
    in_specs = [
        pl.BlockSpec((dec_seq, hd), lambda b, s, pt: (b, 0)),
        pl.BlockSpec((1, hd), lambda b, s, pt: (0, 0)),
    ]
    in_specs += [pl.BlockSpec(page_block, page(jj)) for jj in range(ppb)]
    in_specs += [pl.BlockSpec(page_block, page(jj)) for jj in range(ppb)]
    in_specs += [
        pl.BlockSpec((1, B_HEADS, ppb * PAGE_SIZE), lambda b, s, pt: (b, 0, s)),
        pl.BlockSpec((1, B_HEADS, PAGE_SIZE), lambda b, s, pt: (b, 0, 0)),
        pl.BlockSpec((1, nrow, 1), lambda b, s, pt: (b, 0, 0)),
        pl.BlockSpec((1, hd, dec_seq), lambda b, s, pt: (b, 0, 0)),
        pl.BlockSpec((1, hd, dec_seq), lambda b, s, pt: (b, 0, 0)),
    ]
    grid_spec = pltpu.PrefetchScalarGridSpec(
        num_scalar_prefetch=1,
        grid=(batch, n_steps),
        in_specs=in_specs,
        out_specs=pl.BlockSpec((dec_seq, hd), lambda b, s, pt: (b, 0)),
        scratch_shapes=[
            pltpu.VMEM((nrow, hd), BF16),
            pltpu.VMEM((nrow, 1), F32),
            pltpu.VMEM((nrow, 1), F32),
            pltpu.VMEM((nrow, hd), F32),
            pltpu.VMEM((nrow, ppb * PAGE_SIZE), F32),
            pltpu.VMEM((hd, PAGE_SIZE), F32),
            pltpu.VMEM((hd, PAGE_SIZE), F32),
        ],
    )
    args = ([page_table_flat, hq, qg_full] + [cache_kt] * ppb + [cache_vt] * ppb
            + [ck_past, ck_new, cq_col, kt_new, vt_new])
    return pl.pallas_call(
        functools.partial(_fox_decode_kernel, ppb=ppb, dec_seq=dec_seq, n_steps=n_steps),
        grid_spec=grid_spec,
        out_shape=jax.ShapeDtypeStruct((batch * dec_seq, hd), F32),
        compiler_params=_params("parallel", "arbitrary"),
        name="fox_decode",
    )(*args)


def _trunk(x, batch, seq, states, w, attention, *, tm, kv_tm, kv_rows, gla_cfg):
    n_a = w["a_w_in"].shape[0]
    depth = w["wgu"].shape[0]
    new_states = []
    kv = ctx = None
    for layer in range(depth):
        if layer < n_a:
            h = _norm_matmul(x, w["a_norm"][layer:layer + 1], w["a_w_in"], layer, tm=tm, tn=2048)
            s0 = None if states is None else states[layer]
            og, s_new = _gla(h, w["a_lb"], w["a_onorm"][layer:layer + 1], s0, layer=layer, batch=batch, seq=seq, **gla_cfg)
            new_states.append(s_new)
            mix, wo3, wo_layer = og, w["a_w_out"], layer
        else:
            j = layer - n_a
            if j == 0:
                kv = _shared_kv(x, w["kv_norm"], w["wkt"], w["wvt"], w["wft"], w["bf"], w["kg"],
                                batch=kv_rows[0], seq=kv_rows[1], tm=kv_tm, with_rows=attention.with_rows)
                ctx = attention.prepare(kv)
            mix, wo3, wo_layer = attention.run(j, x, w, ctx, kv), w["b_wo"], j
        x = _ffn(mix, wo3, wo_layer, x, w["ffn_norm"][layer:layer + 1], w["wgu"], w["wd"], layer, tm=min(tm, 512))
    return x, jnp.stack(new_states), kv


class _PromptAttention:
    with_rows = False

    def __init__(self, batch, seq, tq, tk):
        self.batch, self.seq, self.tq, self.tk = batch, seq, tq, tk

    def prepare(self, kv):
        return _cumsum_prompt(kv[2])

    def run(self, j, x, w, ctx, kv):
        crow, ccol = ctx
        qt = _qproj_t(x, w["b_norm"][j:j + 1], w["b_wq_t"], j, w["qg_col"][j], tm=512)
        return _fox_prefill(qt, kv[3], kv[4], ccol, crow, batch=self.batch, seq=self.seq, tq=self.tq, tk=self.tk)


class _SampleAttention:
    with_rows = True

    def __init__(self, batch, dec_seq, n_pages, pt_flat, cache_kt, cache_vt, cache_lft, ppb):
        self.batch, self.dec_seq, self.n_pages, self.pt = batch, dec_seq, n_pages, pt_flat
        self.cache_kt, self.cache_vt, self.cache_lft, self.ppb = cache_kt, cache_vt, cache_lft, ppb

    def _per_batch(self, t):
        rows = t.shape[0]
        return t.reshape(rows, self.batch, self.dec_seq).transpose(1, 0, 2)

    def prepare(self, kv):
        hd = B_HEADS * B_HEAD_DIM
        lft_new = self._per_batch(kv[2][0])
        ck_past, ck_new = _cumsum_sample(self.pt, self.cache_lft, lft_new, batch=self.batch, n_pages=self.n_pages,
                                         dec_seq=self.dec_seq)
        cq_col = ck_new[:, :, :self.dec_seq].reshape(self.batch, B_HEADS * self.dec_seq, 1)
        kt_new = self._per_batch(kv[0].reshape(hd, -1))
        vt_new = self._per_batch(kv[1].reshape(hd, -1))
        return ck_past, ck_new, cq_col, kt_new, vt_new

    def run(self, j, x, w, ctx, kv):
        ck_past, ck_new, cq_col, kt_new, vt_new = ctx
        hq = _norm_matmul(x, w["b_norm"][j:j + 1], w["b_wq"], j, tm=x.shape[0], tn=1024)
        return _fox_decode(self.pt, hq, w["qg_full"][j:j + 1], self.cache_kt, self.cache_vt, ck_past, ck_new, cq_col,
                           kt_new, vt_new, batch=self.batch, n_pages=self.n_pages, dec_seq=self.dec_seq, ppb=self.ppb)


def kernel(x_prompt, x_sample, state_hgrn, cache_k, cache_v, cache_logf, page_table, a_norm, a_w_in, a_lb, a_onorm, a_w_out, kv_norm, w_kvf, b_f, k_norm, b_norm, b_wq, q_norm, b_wo, ffn_norm, w_gate_up, w_down):
    batch, seq, d = x_prompt.shape
    dec_batch, dec_seq, _ = x_sample.shape
    n_pages = page_table.shape[1]
    hd = B_HEADS * B_HEAD_DIM
    assert seq % A_CHUNK == 0 and dec_seq <= A_CHUNK

    w_kvf_t = w_kvf.T.astype(BF16)
    w = dict(
        a_norm=a_norm, a_w_in=a_w_in.astype(BF16), a_lb=a_lb.astype(F32), a_onorm=a_onorm,
        a_w_out=a_w_out.astype(BF16), kv_norm=kv_norm.reshape(1, d),
        wkt=w_kvf_t[:hd], wvt=w_kvf_t[hd:2 * hd], wft=w_kvf_t[2 * hd:],
        bf=b_f.astype(F32).reshape(B_HEADS, 1), kg=k_norm.astype(F32).reshape(B_HEAD_DIM, 1),
        b_norm=b_norm, b_wq=b_wq.astype(BF16), b_wo=b_wo.astype(BF16), ffn_norm=ffn_norm,
        wgu=w_gate_up.astype(BF16), wd=w_down.astype(BF16),
    )
    w["b_wq_t"] = jnp.swapaxes(b_wq, 1, 2).astype(BF16)
    w["qg_col"] = q_norm.astype(F32)[:, :, None]
    w["qg_full"] = jnp.tile(q_norm, (1, B_HEADS))

    prompt_att = _PromptAttention(batch, seq, tq=1024, tk=512)
    y_p, hgrn_p, kv_p = _trunk(
        x_prompt.reshape(batch * seq, d), batch, seq, None, w, prompt_att, tm=1024, kv_tm=512, kv_rows=(batch, seq),
        gla_cfg=dict(c=A_CHUNK, u=A_SUB, t=512, hps=8, out_dtype=BF16))
    k_p = kv_p[0].transpose(0, 3, 1, 2)
    v_p = kv_p[1].transpose(0, 3, 1, 2)
    lf_p = kv_p[2].transpose(0, 2, 1)

    m_s = dec_batch * dec_seq
    pt_flat = page_table.reshape(-1).astype(jnp.int32)
    sample_att = _SampleAttention(dec_batch, dec_seq, n_pages, pt_flat,
                                  cache_k.transpose(0, 2, 3, 1), cache_v.transpose(0, 2, 3, 1),
                                  cache_logf.transpose(0, 2, 1), ppb=16)
    y_s, hgrn_s, kv_s = _trunk(
        x_sample.reshape(m_s, d), dec_batch, dec_seq, state_hgrn, w, sample_att, tm=m_s, kv_tm=m_s, kv_rows=(1, m_s),
        gla_cfg=dict(c=dec_seq, u=min(A_SUB, dec_seq), t=dec_seq, hps=A_HEADS, out_dtype=F32))
    k_s = kv_s[5].reshape(dec_batch, dec_seq, B_HEADS, B_HEAD_DIM)
    v_s = kv_s[6].reshape(dec_batch, dec_seq, B_HEADS, B_HEAD_DIM)
    lf_s = kv_s[2][0].T.reshape(dec_batch, dec_seq, B_HEADS)

    return (y_p.reshape(batch, seq, d), y_s.reshape(dec_batch, dec_seq, d), hgrn_p, k_p, v_p, lf_p,
            hgrn_s, k_s, v_s, lf_s)
```

```python
import functools

import jax
import jax.numpy as jnp
from jax import lax
from jax.experimental import pallas as pl
from jax.experimental.pallas import tpu as pltpu

F32 = jnp.float32
BF16 = jnp.bfloat16
HIGHEST = lax.Precision.HIGHEST

D_MODEL = 1024
A_HEADS = 8
A_DK = 128
A_DV = 128
A_CHUNK = 64
A_SUB = 16
B_HEADS = 16
B_HEAD_DIM = 64
PAGE_SIZE = 128
RMS_EPS = 1e-6
MASK_VALUE = -1e30
MIN_F = 1e-30
LOG2E = 1.4426950408889634
LANES = 128
SUBLANES = 8
VMEM_LIMIT = 56 * 1024 * 1024

_NT = (((1,), (1,)), ((), ()))
_TN = (((0,), (0,)), ((), ()))


def _params(*sem):
    return pltpu.CompilerParams(dimension_semantics=sem, vmem_limit_bytes=VMEM_LIMIT)


def _rms(x, g):
    return x * lax.rsqrt(jnp.mean(x * x, axis=-1, keepdims=True) + RMS_EPS) * g


def _tri_matmul(x, tri, *, tri_on_left):
    hi = x.astype(BF16)
    r1 = x - hi.astype(F32)
    mid = r1.astype(BF16)
    lo = (r1 - mid.astype(F32)).astype(BF16)
    t = tri.astype(BF16)
    out = None
    for piece in (hi, mid, lo):
        d = jnp.dot(t, piece, preferred_element_type=F32) if tri_on_left else jnp.dot(piece, t, preferred_element_type=F32)
        out = d if out is None else out + d
    return out


def _resident(shape):
    return pl.BlockSpec(shape, lambda *_: (0,) * len(shape), pipeline_mode=pl.Buffered(1))


def _resident_layer(w3, layer, col_block=0, cols=None):
    _, r, c = w3.shape
    return pl.BlockSpec((None, r, cols or c), lambda *_: (layer, 0, col_block), pipeline_mode=pl.Buffered(1))


def _group_indicator(n, groups_padded, group):
    r = lax.broadcasted_iota(jnp.int32, (n, groups_padded), 0)
    c = lax.broadcasted_iota(jnp.int32, (n, groups_padded), 1)
    return (r // group == c).astype(F32)


def _head_rms(x, gain_tiled):
    n = x.shape[-1]
    g = _group_indicator(n, LANES, B_HEAD_DIM)
    ss = jnp.dot(x * x, g, preferred_element_type=F32, precision=HIGHEST)
    rs = lax.rsqrt(ss * (1.0 / B_HEAD_DIM) + RMS_EPS)
    rs_full = lax.dot_general(rs, g, _NT, preferred_element_type=F32, precision=HIGHEST)
    return x * rs_full * gain_tiled


def _norm_matmul_kernel(x_ref, g_ref, w_ref, o_ref, xn_ref):
    @pl.when(pl.program_id(1) == 0)
    def _():
        xn_ref[...] = _rms(x_ref[...], g_ref[...]).astype(BF16)

    o_ref[...] = jnp.dot(xn_ref[...], w_ref[...], preferred_element_type=F32).astype(o_ref.dtype)


def _norm_matmul(x, g, w3, layer, *, tm, tn, out_dtype=F32):
    m, k = x.shape
    n = w3.shape[2]
    return pl.pallas_call(
        _norm_matmul_kernel,
        grid=(m // tm, n // tn),
        in_specs=[
            pl.BlockSpec((tm, k), lambda i, j: (i, 0)),
            pl.BlockSpec((1, k), lambda i, j: (0, 0)),
            _resident_layer(w3, layer) if tn == n else pl.BlockSpec((None, k, tn), lambda i, j: (layer, 0, j)),
        ],
        out_specs=pl.BlockSpec((tm, tn), lambda i, j: (i, j)),
        out_shape=jax.ShapeDtypeStruct((m, n), out_dtype),
        scratch_shapes=[pltpu.VMEM((tm, k), BF16)],
        compiler_params=_params("parallel", "arbitrary"),
        name="norm_matmul",
    )(x, g, w3)


def _ffn_kernel(a_ref, wo_ref, r_ref, g_ref, wg_ref, wu_ref, wd_ref, o_ref, h_ref, act_ref, *, tf):
    x = r_ref[...] + jnp.dot(a_ref[...].astype(BF16), wo_ref[...], preferred_element_type=F32)
    h_ref[...] = _rms(x, g_ref[...]).astype(BF16)
    for j in range(act_ref.shape[1] // tf):
        sl = slice(j * tf, (j + 1) * tf)
        a = jnp.dot(h_ref[...], wg_ref[:, sl], preferred_element_type=F32)
        b = jnp.dot(h_ref[...], wu_ref[:, sl], preferred_element_type=F32)
        act_ref[:, sl] = (jax.nn.silu(a) * b).astype(BF16)
    o_ref[...] = x + jnp.dot(act_ref[...], wd_ref[...], preferred_element_type=F32)


def _ffn(a, wo3, wo_layer, res, g, wgu3, wd3, layer, *, tm, tf=256):
    m, d = res.shape
    dff = wd3.shape[1]
    return pl.pallas_call(
        functools.partial(_ffn_kernel, tf=tf),
        grid=(m // tm,),
        in_specs=[
            pl.BlockSpec((tm, a.shape[1]), lambda i: (i, 0)),
            _resident_layer(wo3, wo_layer),
            pl.BlockSpec((tm, d), lambda i: (i, 0)),
            pl.BlockSpec((1, d), lambda i: (0, 0)),
            _resident_layer(wgu3, layer, 0, dff),
            _resident_layer(wgu3, layer, 1, dff),
            _resident_layer(wd3, layer),
        ],
        out_specs=pl.BlockSpec((tm, d), lambda i: (i, 0)),
        out_shape=jax.ShapeDtypeStruct((m, d), F32),
        scratch_shapes=[pltpu.VMEM((tm, d), BF16), pltpu.VMEM((tm, dff), BF16)],
        compiler_params=_params("parallel"),
        name="ffn",
    )(a, wo3, res, g, wgu3, wgu3, wd3)


def _gla_gates(z, lb, c):
    sig = jax.nn.sigmoid(z)
    f = lb + (1.0 - lb) * sig
    logf = jnp.log(jnp.maximum(f, MIN_F))
    k = (1.0 - lb) * jax.nn.sigmoid(-z)
    tri = lax.broadcasted_iota(jnp.int32, (c, c), 0) >= lax.broadcasted_iota(jnp.int32, (c, c), 1)
    return k, _tri_matmul(logf, tri, tri_on_left=True)


def _gla_diag(q, b2, b_scr, k_scr, v_scr, *, c, u):
    rows = lax.broadcasted_iota(jnp.int32, (SUBLANES, 1), 0)
    parts = []
    for i in range(c // u):
        lo = u * i
        for j in range(u // SUBLANES):
            p0 = lo + j * SUBLANES
            qp = q[p0:p0 + SUBLANES]
            bp = b2[p0:p0 + SUBLANES]
            acc = jnp.zeros((SUBLANES, A_DV), F32)
            for s in range(lo, p0 + SUBLANES):
                d = bp - b_scr[s:s + 1, :]
                if s >= p0:
                    d = jnp.minimum(d, 0.0)
                a = jnp.sum(qp * k_scr[s:s + 1, :] * jnp.exp2(d), axis=-1, keepdims=True)
                if s >= p0:
                    a = jnp.where(rows >= s - p0, a, 0.0)
                acc = acc + a * v_scr[s:s + 1, :]
            parts.append(acc)
    return parts


def _gla_mxu(q, k, b, v, st, parts, *, c, u):
    vb = v.astype(BF16)
    o_state = lax.dot_general((q * jnp.exp(b)).astype(BF16), st.astype(BF16), _NT, preferred_element_type=F32)
    parts = list(parts)
    for i in range(1, c // u):
        lo = u * i
        r = b[lo - 1:lo, :]
        qt = (q[lo:lo + u] * jnp.exp(b[lo:lo + u] - r)).astype(BF16)
        kt = (k[:lo] * jnp.exp(r - b[:lo])).astype(BF16)
        att = lax.dot_general(qt, kt, _NT, preferred_element_type=F32)
        off = jnp.dot(att.astype(BF16), vb[:lo], preferred_element_type=F32)
        for j in range(u // SUBLANES):
            idx = (lo + j * SUBLANES) // SUBLANES
            parts[idx] = parts[idx] + off[j * SUBLANES:(j + 1) * SUBLANES]
    o = (parts[0] if len(parts) == 1 else jnp.concatenate(parts, axis=0)) + o_state
    bl = b[c - 1:c, :]
    kd = (k * jnp.exp(bl - b)).astype(BF16)
    st_new = st * jnp.exp(bl) + lax.dot_general(vb, kd, _TN, preferred_element_type=F32)
    return o, st_new


def _gla_kernel(*refs, layer, c, u, n_chunks, hps, has_s0):
    if has_s0:
        q_ref, z_ref, v_ref, gt_ref, lb_ref, og_ref, s0_ref, o_ref, so_ref, st_ref, b_scr, k_scr, v_scr = refs
    else:
        q_ref, z_ref, v_ref, gt_ref, lb_ref, og_ref, o_ref, so_ref, st_ref, b_scr, k_scr, v_scr = refs
        s0_ref = None
    i = pl.program_id(2)

    @pl.when(i == 0)
    def _():
        for hh in range(hps):
            st_ref[hh] = s0_ref[0, hh].T if has_s0 else jnp.zeros((A_DV, A_DK), F32)

    a = lb_ref[...]
    e = jnp.exp(a - jnp.max(a, axis=0, keepdims=True))
    p = e / jnp.sum(e, axis=0, keepdims=True)
    lb_all = jnp.sum(p[:layer + 1], axis=0, keepdims=True) - p[0:1]
    og = og_ref[...]

    def body(ci, carry):
        rs = pl.ds(pl.multiple_of(ci * c, c), c)
        cols = [slice(hh * LANES, (hh + 1) * LANES) for hh in range(hps)]
        kb = [_gla_gates(z_ref[rs, cs], lb_all[:, cs], c) for cs in cols]
        for hh, cs in enumerate(cols):
            b_scr[hh] = kb[hh][1] * LOG2E
            k_scr[hh] = kb[hh][0]
            v_scr[hh] = v_ref[rs, cs]
        parts = [_gla_diag(q_ref[rs, cs], b_scr[hh], b_scr.at[hh], k_scr.at[hh], v_scr.at[hh], c=c, u=u)
                 for hh, cs in enumerate(cols)]
        for hh, cs in enumerate(cols):
            o, st_new = _gla_mxu(q_ref[rs, cs], kb[hh][0], kb[hh][1], v_ref[rs, cs], st_ref[hh], parts[hh], c=c, u=u)
            st_ref[hh] = st_new
            on = o * lax.rsqrt(jnp.mean(o * o, axis=-1, keepdims=True) + RMS_EPS) * og
            o_ref[rs, cs] = (on * jax.nn.silu(gt_ref[rs, cs])).astype(o_ref.dtype)
        return carry

    if n_chunks == 1:
        body(0, 0)
    else:
        lax.fori_loop(0, n_chunks, body, 0)

    @pl.when(i == pl.num_programs(2) - 1)
    def _():
        for hh in range(hps):
            so_ref[0, hh] = st_ref[hh].T


def _gla(h, a_lb, o_gain, s0, *, layer, batch, seq, c, u, t, hps, out_dtype):
    m = batch * seq
    nt = seq // t
    hb = A_HEADS // hps
    w = hps * LANES
    kd = A_HEADS * A_DK
    nblk = kd // w

    def col(off):
        return lambda b, hg, i: (b * nt + i, off * nblk + hg)

    in_specs = [
        pl.BlockSpec((t, w), col(0)),
        pl.BlockSpec((t, w), col(1)),
        pl.BlockSpec((t, w), col(2)),
        pl.BlockSpec((t, w), col(3)),
        pl.BlockSpec((a_lb.shape[0], w), lambda b, hg, i: (0, hg)),
        pl.BlockSpec((1, A_DV), lambda b, hg, i: (0, 0)),
    ]
    args = [h, h, h, h, a_lb, o_gain]
    if s0 is not None:
        in_specs.append(pl.BlockSpec((1, hps, A_DK, A_DV), lambda b, hg, i: (b, hg, 0, 0)))
        args.append(s0)
    kern = functools.partial(_gla_kernel, layer=layer, c=c, u=u, n_chunks=t // c, hps=hps, has_s0=s0 is not None)
    return pl.pallas_call(
        kern,
        grid=(batch, hb, nt),
        in_specs=in_specs,
        out_specs=[
            pl.BlockSpec((t, w), lambda b, hg, i: (b * nt + i, hg)),
            pl.BlockSpec((1, hps, A_DK, A_DV), lambda b, hg, i: (b, hg, 0, 0)),
        ],
        out_shape=[
            jax.ShapeDtypeStruct((m, kd), out_dtype),
            jax.ShapeDtypeStruct((batch, A_HEADS, A_DK, A_DV), F32),
        ],
        scratch_shapes=[
            pltpu.VMEM((hps, A_DV, A_DK), F32),
            pltpu.VMEM((hps, c, A_DK), F32),
            pltpu.VMEM((hps, c, A_DK), F32),
            pltpu.VMEM((hps, c, A_DV), F32),
        ],
        compiler_params=_params("parallel", "parallel", "arbitrary"),
        name="hgrn2_gla",
    )(*args)


def _kv_kernel(x_ref, g_ref, wkt_ref, wvt_ref, wft_ref, bf_ref, kg_ref, *out_refs, with_rows):
    kt_ref, vt_ref, lft_ref, kb_ref, vbt_ref = out_refs[:5]
    h = _rms(x_ref[...], g_ref[...]).astype(BF16)
    tm = h.shape[0]
    kt = lax.dot_general(wkt_ref[...], h, _NT, preferred_element_type=F32)
    k3 = kt.reshape(B_HEADS, B_HEAD_DIM, tm)
    ms = jnp.mean(k3 * k3, axis=1, keepdims=True)
    k3 = k3 * lax.rsqrt(ms + RMS_EPS) * kg_ref[...][None]
    vt = lax.dot_general(wvt_ref[...], h, _NT, preferred_element_type=F32)
    fz = lax.dot_general(wft_ref[...], h, _NT, preferred_element_type=F32) + bf_ref[...]
    krow = k3.reshape(B_HEADS * B_HEAD_DIM, tm).T
    kt_ref[0] = k3
    vt_ref[0] = vt.reshape(B_HEADS, B_HEAD_DIM, tm)
    lft_ref[0] = jnp.minimum(fz, 0.0) - jnp.log1p(jnp.exp(-jnp.abs(fz)))
    kb_ref[...] = krow.astype(BF16)
    vbt_ref[0] = vt.reshape(B_HEADS, B_HEAD_DIM, tm).astype(BF16)
    if with_rows:
        krow_ref, vrow_ref = out_refs[5:]
        krow_ref[...] = krow
        vrow_ref[...] = vt.T


def _shared_kv(x, g, wkt, wvt, wft, bf, kg, *, batch, seq, tm, with_rows):
    m, d = x.shape
    hd = B_HEADS * B_HEAD_DIM
    nt = seq // tm
    t_spec = pl.BlockSpec((1, B_HEADS, B_HEAD_DIM, tm), lambda b, i: (b, 0, 0, i))
    row_spec = pl.BlockSpec((tm, hd), lambda b, i: (b * nt + i, 0))
    t_shape = (batch, B_HEADS, B_HEAD_DIM, seq)
    out_specs = [t_spec, t_spec, pl.BlockSpec((1, B_HEADS, tm), lambda b, i: (b, 0, i)), row_spec, t_spec]
    out_shape = [
        jax.ShapeDtypeStruct(t_shape, F32),
        jax.ShapeDtypeStruct(t_shape, F32),
        jax.ShapeDtypeStruct((batch, B_HEADS, seq), F32),
        jax.ShapeDtypeStruct((m, hd), BF16),
        jax.ShapeDtypeStruct(t_shape, BF16),
    ]
    if with_rows:
        out_specs += [row_spec, row_spec]
        out_shape += [jax.ShapeDtypeStruct((m, hd), F32)] * 2
    return pl.pallas_call(
        functools.partial(_kv_kernel, with_rows=with_rows),
        grid=(batch, nt),
        in_specs=[
            pl.BlockSpec((tm, d), lambda b, i: (b * nt + i, 0)),
            pl.BlockSpec((1, d), lambda b, i: (0, 0)),
            _resident((hd, d)),
            _resident((hd, d)),
            _resident((B_HEADS, d)),
            pl.BlockSpec((B_HEADS, 1), lambda b, i: (0, 0)),
            pl.BlockSpec((B_HEAD_DIM, 1), lambda b, i: (0, 0)),
        ],
        out_specs=out_specs,
        out_shape=out_shape,
        compiler_params=_params("parallel", "parallel"),
        name="shared_kv",
    )(x, g, wkt, wvt, wft, bf, kg)


def _qproj_kernel(x_ref, g_ref, wt_ref, qg_ref, o_ref):
    h = _rms(x_ref[...], g_ref[...]).astype(BF16)
    tm = h.shape[0]
    qt = lax.dot_general(wt_ref[...], h, _NT, preferred_element_type=F32)
    q3 = qt.reshape(B_HEADS, B_HEAD_DIM, tm)
    ms = jnp.mean(q3 * q3, axis=1, keepdims=True)
    q3 = q3 * lax.rsqrt(ms + RMS_EPS) * qg_ref[...][None] * (B_HEAD_DIM ** -0.5)
    o_ref[...] = q3.reshape(B_HEADS * B_HEAD_DIM, tm).astype(BF16)


def _qproj_t(x, g, wt3, layer, qg, *, tm):
    m, d = x.shape
    hd = wt3.shape[1]
    return pl.pallas_call(
        _qproj_kernel,
        grid=(m // tm,),
        in_specs=[
            pl.BlockSpec((tm, d), lambda i: (i, 0)),
            pl.BlockSpec((1, d), lambda i: (0, 0)),
            _resident_layer(wt3, layer),
            pl.BlockSpec((B_HEAD_DIM, 1), lambda i: (0, 0)),
        ],
        out_specs=pl.BlockSpec((hd, tm), lambda i: (0, i)),
        out_shape=jax.ShapeDtypeStruct((hd, m), BF16),
        compiler_params=_params("parallel"),
        name="qproj_t",
    )(x, g, wt3, qg)


def _cumsum_prompt_kernel(x_ref, row_ref, col_ref, carry_scr):
    @pl.when(pl.program_id(1) == 0)
    def _():
        carry_scr[...] = jnp.zeros_like(carry_scr)

    x = x_ref[0]
    t = x.shape[1]
    upper = lax.broadcasted_iota(jnp.int32, (t, t), 0) <= lax.broadcasted_iota(jnp.int32, (t, t), 1)
    row = _tri_matmul(x, upper, tri_on_left=False) + carry_scr[...]
    row_ref[0] = row
    col_ref[0] = row.T
    carry_scr[...] = row[:, t - 1:t]


def _cumsum_prompt(lft, *, t=256):
    b, h, l = lft.shape
    return pl.pallas_call(
        _cumsum_prompt_kernel,
        grid=(b, l // t),
        in_specs=[pl.BlockSpec((1, h, t), lambda i, j: (i, 0, j))],
        out_specs=[
            pl.BlockSpec((1, h, t), lambda i, j: (i, 0, j)),
            pl.BlockSpec((1, t, h), lambda i, j: (i, j, 0)),
        ],
        out_shape=[jax.ShapeDtypeStruct((b, h, l), F32), jax.ShapeDtypeStruct((b, l, h), F32)],
        scratch_shapes=[pltpu.VMEM((h, 1), F32)],
        compiler_params=_params("parallel", "arbitrary"),
        name="cumsum_prompt",
    )(lft)


def _fox_prefill_kernel(qt_ref, k_ref, vt_ref, ccol_ref, crow_ref, o_ref, m_scr, l_scr, acc_scr, *, tq, tk, hpg):
    grp = pl.program_id(1)
    qi = pl.program_id(2)
    row_head = lax.broadcasted_iota(jnp.int32, (2 * B_HEAD_DIM, 1), 0) // B_HEAD_DIM
    qm = []
    for pr in range(hpg // 2):
        qt = qt_ref[pr * 2 * B_HEAD_DIM:(pr + 1) * 2 * B_HEAD_DIM, :]
        qm += [jnp.where(row_head == hh, qt, jnp.zeros_like(qt)) for hh in range(2)]
    hl = lax.broadcasted_iota(jnp.int32, (1, B_HEADS), 1)
    nsub = tq // tk

    m_scr[...] = jnp.full_like(m_scr, -jnp.inf)
    l_scr[...] = jnp.zeros_like(l_scr)
    acc_scr[...] = jnp.zeros_like(acc_scr)

    def step(kb, diag):
        ks = pl.ds(pl.multiple_of(kb * tk, tk), tk)
        ccol = ccol_ref[0, ks, :]
        q0 = diag or 0
        qs = slice(q0, tq)
        ys = []
        for hh in range(hpg):
            pr = hh // 2
            k = k_ref[ks, pr * 2 * B_HEAD_DIM:(pr + 1) * 2 * B_HEAD_DIM]
            ys.append(jnp.dot(k, qm[hh][:, qs], preferred_element_type=F32))
        ps = []
        alphas = []
        for hh in range(hpg):
            ck = jnp.sum(jnp.where(hl == hpg * grp + hh, ccol, 0.0), axis=-1, keepdims=True)
            y = ys[hh] - ck
            if diag is not None:
                r = lax.broadcasted_iota(jnp.int32, (tk, tq - q0), 0)
                c = lax.broadcasted_iota(jnp.int32, (tk, tq - q0), 1)
                y = jnp.where(r <= c, y, MASK_VALUE)
            cq = crow_ref[0, 0, hh:hh + 1, qs]
            m_old = m_scr[hh, :, qs]
            m_new = jnp.maximum(m_old, jnp.max(y, axis=0, keepdims=True) + cq)
            alpha = jnp.exp(m_old - m_new)
            p = jnp.exp(y - (m_new - cq))
            l_scr[hh, :, qs] = alpha * l_scr[hh, :, qs] + jnp.sum(p, axis=0, keepdims=True)
            m_scr[hh, :, qs] = m_new
            ps.append(p.astype(BF16))
            alphas.append(alpha)
        for hh in range(hpg):
            vt = vt_ref[0, 0, hh * B_HEAD_DIM:(hh + 1) * B_HEAD_DIM, ks]
            acc_scr[hh, :, qs] = alphas[hh] * acc_scr[hh, :, qs] + jnp.dot(vt, ps[hh], preferred_element_type=F32)

    def body(kb, carry):
        step(kb, None)
        return carry

    lax.fori_loop(0, qi * nsub, body, 0)
    for j in range(nsub):
        step(qi * nsub + j, j * tk)
    ot = jnp.concatenate([acc_scr[hh] / l_scr[hh] for hh in range(hpg)], axis=0)
    o_ref[...] = ot.T.astype(o_ref.dtype)


def _fox_prefill(qt, kb, vbt, ccol, crow, *, batch, seq, tq, tk, hpg=4):
    m = batch * seq
    nq = seq // tq
    ngrp = B_HEADS // hpg
    gw = hpg * B_HEAD_DIM
    vbt4 = vbt.reshape(batch, ngrp, gw, seq)
    crow4 = crow.reshape(batch, ngrp, hpg, seq)
    return pl.pallas_call(
        functools.partial(_fox_prefill_kernel, tq=tq, tk=tk, hpg=hpg),
        grid=(batch, ngrp, nq),
        in_specs=[
            pl.BlockSpec((gw, tq), lambda b, p, i: (p, b * nq + i)),
            pl.BlockSpec((seq, gw), lambda b, p, i: (b, p)),
            pl.BlockSpec((1, 1, gw, seq), lambda b, p, i: (b, p, 0, 0)),
            pl.BlockSpec((1, seq, B_HEADS), lambda b, p, i: (b, 0, 0)),
            pl.BlockSpec((1, 1, hpg, tq), lambda b, p, i: (b, p, 0, i)),
        ],
        out_specs=pl.BlockSpec((tq, gw), lambda b, p, i: (b * nq + i, p)),
        out_shape=jax.ShapeDtypeStruct((m, B_HEADS * B_HEAD_DIM), BF16),
        scratch_shapes=[
            pltpu.VMEM((hpg, 1, tq), F32),
            pltpu.VMEM((hpg, 1, tq), F32),
            pltpu.VMEM((hpg, B_HEAD_DIM, tq), F32),
        ],
        compiler_params=_params("parallel", "parallel", "arbitrary"),
        name="fox_prefill",
    )(qt, kb, vbt4, ccol, crow4)


def _cumsum_sample_kernel(*refs, n_steps, ppb):
    page_refs = refs[1:1 + ppb]
    new_ref, past_ref, newrow_ref, carry_scr = refs[1 + ppb:]
    j = pl.program_id(1)
    h = carry_scr.shape[0]

    @pl.when(j == 0)
    def _():
        carry_scr[...] = jnp.zeros_like(carry_scr)

    def local_cumsum(x):
        t = x.shape[1]
        upper = (lax.broadcasted_iota(jnp.int32, (t, PAGE_SIZE), 0)
                 <= lax.broadcasted_iota(jnp.int32, (t, PAGE_SIZE), 1))
        return _tri_matmul(x, upper, tri_on_left=False)

    loc = local_cumsum(jnp.concatenate([r[0] for r in page_refs], axis=0))
    carry = carry_scr[...]
    for jj in range(ppb):
        page = loc[jj * h:(jj + 1) * h]
        past_ref[0, :, jj * PAGE_SIZE:(jj + 1) * PAGE_SIZE] = page + carry
        carry = carry + page[:, PAGE_SIZE - 1:PAGE_SIZE]
    carry_scr[...] = carry

    @pl.when(j == n_steps - 1)
    def _():
        newrow_ref[0] = local_cumsum(new_ref[0]) + carry


def _cumsum_sample(page_table_flat, cache_lft, lft_new, *, batch, n_pages, dec_seq, ppb=32):
    h = cache_lft.shape[1]
    ppb = min(ppb, n_pages)
    n_steps = n_pages // ppb

    def page(jj):
        return lambda b, j, pt: (pt[b * n_pages + j * ppb + jj], 0, 0)

    grid_spec = pltpu.PrefetchScalarGridSpec(
        num_scalar_prefetch=1,
        grid=(batch, n_steps),
        in_specs=[pl.BlockSpec((1, h, PAGE_SIZE), page(jj)) for jj in range(ppb)]
        + [pl.BlockSpec((1, h, dec_seq), lambda b, j, pt: (b, 0, 0))],
        out_specs=[
            pl.BlockSpec((1, h, ppb * PAGE_SIZE), lambda b, j, pt: (b, 0, j)),
            pl.BlockSpec((1, h, PAGE_SIZE), lambda b, j, pt: (b, 0, 0)),
        ],
        scratch_shapes=[pltpu.VMEM((h, 1), F32)],
    )
    return pl.pallas_call(
        functools.partial(_cumsum_sample_kernel, n_steps=n_steps, ppb=ppb),
        grid_spec=grid_spec,
        out_shape=[
            jax.ShapeDtypeStruct((batch, h, n_pages * PAGE_SIZE), F32),
            jax.ShapeDtypeStruct((batch, h, PAGE_SIZE), F32),
        ],
        compiler_params=_params("parallel", "arbitrary"),
        name="cumsum_sample",
    )(page_table_flat, *([cache_lft] * ppb), lft_new)


def _fox_decode_kernel(*refs, ppb, dec_seq, n_steps):
    hq_ref, qg_ref = refs[1:3]
    k_refs = refs[3:3 + ppb]
    v_refs = refs[3 + ppb:3 + 2 * ppb]
    ck_ref, ckn_ref, cq_ref, kn_ref, vn_ref, o_ref = refs[3 + 2 * ppb:9 + 2 * ppb]
    qbd_scr, m_scr, l_scr, acc_scr, bias_scr, knp_scr, vnp_scr = refs[9 + 2 * ppb:]
    s_id = pl.program_id(1)
    hd = B_HEADS * B_HEAD_DIM
    nrow = B_HEADS * dec_seq
    lane_head = lax.broadcasted_iota(jnp.int32, (1, hd), 1) // B_HEAD_DIM

    @pl.when(s_id == 0)
    def _():
        qn = _head_rms(hq_ref[...], qg_ref[...]) * (B_HEAD_DIM ** -0.5)
        for h2 in range(B_HEADS // 2):
            a = jnp.where(lane_head == 2 * h2, qn, 0.0)
            b = jnp.where(lane_head == 2 * h2 + 1, qn, 0.0)
            qbd_scr[2 * h2 * dec_seq:(2 * h2 + 2) * dec_seq, :] = jnp.concatenate([a, b], axis=0).astype(BF16)
        m_scr[...] = jnp.full_like(m_scr, -jnp.inf)
        l_scr[...] = jnp.zeros_like(l_scr)
        acc_scr[...] = jnp.zeros_like(acc_scr)
        knp_scr[...] = jnp.zeros_like(knp_scr)
        vnp_scr[...] = jnp.zeros_like(vnp_scr)
        knp_scr[:, 0:dec_seq] = kn_ref[0]
        vnp_scr[:, 0:dec_seq] = vn_ref[0]

    cq = cq_ref[0]

    def attend(kt_tiles, vt_tiles, ck_rows, mask):
        n = len(kt_tiles)
        qbd = qbd_scr[...]
        groups = [list(range(g, min(g + 2, n))) for g in range(0, n, 2)]
        s = jnp.concatenate(
            [jnp.dot(qbd, jnp.concatenate([kt_tiles[t] for t in g], axis=1), preferred_element_type=F32) for g in groups],
            axis=1)
        for h in range(B_HEADS):
            rs = slice(h * dec_seq, (h + 1) * dec_seq)
            bias_scr[rs, 0:n * PAGE_SIZE] = cq[rs] - ck_rows[h:h + 1, :]
        s = s + bias_scr[:, 0:n * PAGE_SIZE]
        if mask is not None:
            s = jnp.where(mask, s, MASK_VALUE)
        m_old = m_scr[...]
        m_new = jnp.maximum(m_old, jnp.max(s, axis=-1, keepdims=True))
        alpha = jnp.exp(m_old - m_new)
        p = jnp.exp(s - m_new)
        l_scr[...] = alpha * l_scr[...] + jnp.sum(p, axis=-1, keepdims=True)
        pv = None
        for g in groups:
            t = lax.dot_general(p[:, g[0] * PAGE_SIZE:(g[-1] + 1) * PAGE_SIZE].astype(BF16),
                                jnp.concatenate([vt_tiles[t] for t in g], axis=1), _NT, preferred_element_type=F32)
            pv = t if pv is None else pv + t
        acc_scr[...] = alpha * acc_scr[...] + pv
        m_scr[...] = m_new

    attend([r[0].reshape(hd, PAGE_SIZE).astype(BF16) for r in k_refs],
           [r[0].reshape(hd, PAGE_SIZE).astype(BF16) for r in v_refs], ck_ref[0], None)

    @pl.when(s_id == n_steps - 1)
    def _():
        r = lax.broadcasted_iota(jnp.int32, (nrow, PAGE_SIZE), 0) % dec_seq
        c = lax.broadcasted_iota(jnp.int32, (nrow, PAGE_SIZE), 1)
        attend([knp_scr[...].astype(BF16)], [vnp_scr[...].astype(BF16)], ckn_ref[0], c <= r)
        o = acc_scr[...] / l_scr[...]
        out = jnp.zeros((dec_seq, hd), F32)
        for h in range(B_HEADS):
            out = out + jnp.where(lane_head == h, o[h * dec_seq:(h + 1) * dec_seq, :], 0.0)
        o_ref[...] = out.astype(o_ref.dtype)


def _fox_decode(page_table_flat, hq, qg_full, cache_kt, cache_vt, ck_past, ck_new, cq_col, kt_new, vt_new,
                *, batch, n_pages, dec_seq, ppb):
    hd = B_HEADS * B_HEAD_DIM
    n_steps = n_pages // ppb
    nrow = B_HEADS * dec_seq

    def page(jj):
        return lambda b, s, pt: (pt[b * n_pages + s * ppb + jj], 0, 0, 0)

    page_block = (1, B_HEADS, B_HEAD_DIM, PAGE_SIZE)
    in_specs = [
        pl.BlockSpec((dec_seq, hd), lambda b, s, pt: (b, 0)),
        pl.BlockSpec((1, hd), lambda b, s, pt: (0, 0)),
    ]
    in_specs += [pl.BlockSpec(page_block, page(jj)) for jj in range(ppb)]
    in_specs += [pl.BlockSpec(page_block, page(jj)) for jj in range(ppb)]
    in_specs += [
        pl.BlockSpec((1, B_HEADS, ppb * PAGE_SIZE), lambda b, s, pt: (b, 0, s)),
        pl.BlockSpec((1, B_HEADS, PAGE_SIZE), lambda b, s, pt: (b, 0, 0)),
        pl.BlockSpec((1, nrow, 1), lambda b, s, pt: (b, 0, 0)),
        pl.BlockSpec((1, hd, dec_seq), lambda b, s, pt: (b, 0, 0)),
        pl.BlockSpec((1, hd, dec_seq), lambda b, s, pt: (b, 0, 0)),
    ]
    grid_spec = pltpu.PrefetchScalarGridSpec(
        num_scalar_prefetch=1,
        grid=(batch, n_steps),
        in_specs=in_specs,
        out_specs=pl.BlockSpec((dec_seq, hd), lambda b, s, pt: (b, 0)),
        scratch_shapes=[
            pltpu.VMEM((nrow, hd), BF16),
            pltpu.VMEM((nrow, 1), F32),
            pltpu.VMEM((nrow, 1), F32),
            pltpu.VMEM((nrow, hd), F32),
            pltpu.VMEM((nrow, ppb * PAGE_SIZE), F32),
            pltpu.VMEM((hd, PAGE_SIZE), F32),
            pltpu.VMEM((hd, PAGE_SIZE), F32),
        ],
    )
    args = ([page_table_flat, hq, qg_full] + [cache_kt] * ppb + [cache_vt] * ppb
            + [ck_past, ck_new, cq_col, kt_new, vt_new])
    return pl.pallas_call(
        functools.partial(_fox_decode_kernel, ppb=ppb, dec_seq=dec_seq, n_steps=n_steps),
        grid_spec=grid_spec,
        out_shape=jax.ShapeDtypeStruct((batch * dec_seq, hd), F32),
        compiler_params=_params("parallel", "arbitrary"),
        name="fox_decode",
    )(*args)


def _trunk(x, batch, seq, states, w, attention, *, tm, kv_tm, kv_rows, gla_cfg):
    n_a = w["a_w_in"].shape[0]
    depth = w["wgu"].shape[0]
    new_states = []
    kv = ctx = None
    for layer in range(depth):
        if layer < n_a:
            h = _norm_matmul(x, w["a_norm"][layer:layer + 1], w["a_w_in"], layer, tm=min(tm, 512), tn=w["a_w_in"].shape[2])
            s0 = None if states is None else states[layer]
            og, s_new = _gla(h, w["a_lb"], w["a_onorm"][layer:layer + 1], s0, layer=layer, batch=batch, seq=seq, **gla_cfg)
            new_states.append(s_new)
            mix, wo3, wo_layer = og, w["a_w_out"], layer
        else:
            j = layer - n_a
            if j == 0:
                kv = _shared_kv(x, w["kv_norm"], w["wkt"], w["wvt"], w["wft"], w["bf"], w["kg"],
                                batch=kv_rows[0], seq=kv_rows[1], tm=kv_tm, with_rows=attention.with_rows)
                ctx = attention.prepare(kv)
            mix, wo3, wo_layer = attention.run(j, x, w, ctx, kv), w["b_wo"], j
        x = _ffn(mix, wo3, wo_layer, x, w["ffn_norm"][layer:layer + 1], w["wgu"], w["wd"], layer, tm=min(tm, 512))
    return x, jnp.stack(new_states), kv


class _PromptAttention:
    with_rows = False

    def __init__(self, batch, seq, tq, tk):
        self.batch, self.seq, self.tq, self.tk = batch, seq, tq, tk

    def prepare(self, kv):
        return _cumsum_prompt(kv[2])

    def run(self, j, x, w, ctx, kv):
        crow, ccol = ctx
        qt = _qproj_t(x, w["b_norm"][j:j + 1], w["b_wq_t"], j, w["qg_col"][j], tm=512)
        return _fox_prefill(qt, kv[3], kv[4], ccol, crow, batch=self.batch, seq=self.seq, tq=self.tq, tk=self.tk)


class _SampleAttention:
    with_rows = True

    def __init__(self, batch, dec_seq, n_pages, pt_flat, cache_kt, cache_vt, cache_lft, ppb):
        self.batch, self.dec_seq, self.n_pages, self.pt = batch, dec_seq, n_pages, pt_flat
        self.cache_kt, self.cache_vt, self.cache_lft, self.ppb = cache_kt, cache_vt, cache_lft, ppb

    def _per_batch(self, t):
        rows = t.shape[0]
        return t.reshape(rows, self.batch, self.dec_seq).transpose(1, 0, 2)

    def prepare(self, kv):
        hd = B_HEADS * B_HEAD_DIM
        lft_new = self._per_batch(kv[2][0])
        ck_past, ck_new = _cumsum_sample(self.pt, self.cache_lft, lft_new, batch=self.batch, n_pages=self.n_pages,
                                         dec_seq=self.dec_seq)
        cq_col = ck_new[:, :, :self.dec_seq].reshape(self.batch, B_HEADS * self.dec_seq, 1)
        kt_new = self._per_batch(kv[0].reshape(hd, -1))
        vt_new = self._per_batch(kv[1].reshape(hd, -1))
        return ck_past, ck_new, cq_col, kt_new, vt_new

    def run(self, j, x, w, ctx, kv):
        ck_past, ck_new, cq_col, kt_new, vt_new = ctx
        hq = _norm_matmul(x, w["b_norm"][j:j + 1], w["b_wq"], j, tm=x.shape[0], tn=1024)
        return _fox_decode(self.pt, hq, w["qg_full"][j:j + 1], self.cache_kt, self.cache_vt, ck_past, ck_new, cq_col,
                           kt_new, vt_new, batch=self.batch, n_pages=self.n_pages, dec_seq=self.dec_seq, ppb=self.ppb)


def kernel(x_prompt, x_sample, state_hgrn, cache_k, cache_v, cache_logf, page_table, a_norm, a_w_in, a_lb, a_onorm, a_w_out, kv_norm, w_kvf, b_f, k_norm, b_norm, b_wq, q_norm, b_wo, ffn_norm, w_gate_up, w_down):
    batch, seq, d = x_prompt.shape
    dec_batch, dec_seq, _ = x_sample.shape
    n_pages = page_table.shape[1]
    hd = B_HEADS * B_HEAD_DIM
    assert seq % A_CHUNK == 0 and dec_seq <= A_CHUNK

    w_kvf_t = w_kvf.T.astype(BF16)
    w = dict(
        a_norm=a_norm, a_w_in=a_w_in.astype(BF16), a_lb=a_lb.astype(F32), a_onorm=a_onorm,
        a_w_out=a_w_out.astype(BF16), kv_norm=kv_norm.reshape(1, d),
        wkt=w_kvf_t[:hd], wvt=w_kvf_t[hd:2 * hd], wft=w_kvf_t[2 * hd:],
        bf=b_f.astype(F32).reshape(B_HEADS, 1), kg=k_norm.astype(F32).reshape(B_HEAD_DIM, 1),
        b_norm=b_norm, b_wq=b_wq.astype(BF16), b_wo=b_wo.astype(BF16), ffn_norm=ffn_norm,
        wgu=w_gate_up.astype(BF16), wd=w_down.astype(BF16),
    )
    w["b_wq_t"] = jnp.swapaxes(b_wq, 1, 2).astype(BF16)
    w["qg_col"] = q_norm.astype(F32)[:, :, None]
    w["qg_full"] = jnp.tile(q_norm, (1, B_HEADS))

    prompt_att = _PromptAttention(batch, seq, tq=1024, tk=512)
    y_p, hgrn_p, kv_p = _trunk(
        x_prompt.reshape(batch * seq, d), batch, seq, None, w, prompt_att, tm=1024, kv_tm=512, kv_rows=(batch, seq),
        gla_cfg=dict(c=A_CHUNK, u=A_SUB, t=512, hps=8, out_dtype=BF16))
    k_p = kv_p[0].transpose(0, 3, 1, 2)
    v_p = kv_p[1].transpose(0, 3, 1, 2)
    lf_p = kv_p[2].transpose(0, 2, 1)

    m_s = dec_batch * dec_seq
    pt_flat = page_table.reshape(-1).astype(jnp.int32)
    sample_att = _SampleAttention(dec_batch, dec_seq, n_pages, pt_flat,
                                  cache_k.transpose(0, 2, 3, 1), cache_v.transpose(0, 2, 3, 1),
                                  cache_logf.transpose(0, 2, 1), ppb=16)
    y_s, hgrn_s, kv_s = _trunk(
        x_sample.reshape(m_s, d), dec_batch, dec_seq, state_hgrn, w, sample_att, tm=m_s, kv_tm=m_s, kv_rows=(1, m_s),
        gla_cfg=dict(c=dec_seq, u=min(A_SUB, dec_seq), t=dec_seq, hps=A_HEADS, out_dtype=F32))
    k_s = kv_s[5].reshape(dec_batch, dec_seq, B_HEADS, B_HEAD_DIM)
    v_s = kv_s[6].reshape(dec_batch, dec_seq, B_HEADS, B_HEAD_DIM)
    lf_s = kv_s[2][0].T.reshape(dec_batch, dec_seq, B_HEADS)

    return (y_p.reshape(batch, seq, d), y_s.reshape(dec_batch, dec_seq, d), hgrn_p, k_p, v_p, lf_p,
            hgrn_s, k_s, v_s, lf_s)
```
